```python
import functools
import jax, jax.numpy as jnp
from jax import lax
import numpy as np

D_MODEL = 2048
BATCH = 4
SEQ = 2048
DEPTH = 4

GRID_W = 64
CTX_LEN = 256
N_MIXERS = 2
RET_HEADS = D_MODEL // 256
RET_DK = 256
RET_DV = 2 * RET_DK
RET_CHUNK = 128
RET_QK = RET_HEADS * RET_DK
RET_VW = RET_HEADS * RET_DV
RET_IN = 2 * RET_QK + 2 * RET_VW
ROPE_BASE = 10000.0
DN_K_HEADS = D_MODEL // 128
DN_V_HEADS = 2 * DN_K_HEADS
DN_DK = 128
DN_DV = 128
DN_CHUNK = 64
DN_CONV = 5
DN_QK = DN_K_HEADS * DN_DK
DN_VW = DN_V_HEADS * DN_DV
DN_CONV_CH = 2 * DN_QK + DN_VW
DN_IN = DN_CONV_CH + DN_VW + 4 * DN_V_HEADS
FFN_HIDDEN = 5504
N_MOD = 9
ALPHA = (2 * DEPTH) ** 0.25
BETA = (8 * DEPTH) ** -0.25
LN_EPS = 1e-5
NORM_EPS = 1e-6

kernel_name = "hybrid_retention_gated_deltanet_dit"


def _standardize(t):
    tf = t.astype(jnp.float32)
    mu = tf.mean(-1, keepdims=True)
    var = jnp.mean(jnp.square(tf - mu), -1, keepdims=True)
    return (tf - mu) * lax.rsqrt(var + LN_EPS)


def layer_norm(x, g, b):
    return (_standardize(x) * g.astype(jnp.float32) + b.astype(jnp.float32)).astype(x.dtype)


def _rms(t):
    tf = t.astype(jnp.float32)
    return tf * lax.rsqrt(jnp.mean(jnp.square(tf), -1, keepdims=True) + NORM_EPS)


def _l2norm(t):
    tf = t.astype(jnp.float32)
    return tf * lax.rsqrt(jnp.sum(jnp.square(tf), -1, keepdims=True) + NORM_EPS)


def _heads(t, n_heads, d):
    b, l = t.shape[:2]
    return t.reshape(b, l, n_heads, d).transpose(0, 2, 1, 3)


def _merge_heads(t):
    b, h, l, d = t.shape
    return t.transpose(0, 2, 1, 3).reshape(b, l, h * d)


def ada_modulation(cond, w, b):
    m = jax.nn.silu(cond) @ w + b
    return m.reshape(cond.shape[:-1] + (N_MOD, D_MODEL))


def mod_terms(m, j):
    pick = lambda r: m[..., r, :][..., None, :]
    return pick(3 * j), pick(3 * j + 1), pick(3 * j + 2)


def swiglu(h, w_in, w_out):
    gate, up = jnp.split(h @ w_in, 2, axis=-1)
    return (jax.nn.silu(gate) * up) @ w_out


def ffn_step(h, m, j, w_in, w_out, g, b):
    shift, scale, gate = mod_terms(m, j)
    y = swiglu(h * (1 + scale) + shift, w_in, w_out)
    return layer_norm(ALPHA * h + 0.5 * gate * y, g, b)


def axial_rotary(length):
    rows = length // GRID_W
    pos_r = jnp.repeat(jnp.arange(rows), GRID_W).astype(jnp.float32)
    pos_c = jnp.tile(jnp.arange(GRID_W), rows).astype(jnp.float32)
    half = RET_DK // 2
    inv = ROPE_BASE ** (-jnp.arange(0, half, 2, dtype=jnp.float32) / half)
    ang = jnp.concatenate([pos_r[:, None] * inv, pos_c[:, None] * inv], -1)
    return jnp.cos(ang), jnp.sin(ang)


def apply_rotary(t, cos, sin):
    t1, t2 = t[..., 0::2], t[..., 1::2]
    return jnp.stack([t1 * cos - t2 * sin, t1 * sin + t2 * cos], -1).reshape(t.shape)


def short_conv(x, w):
    ch = x.shape[-1]
    return lax.conv_general_dilated(
        x, w[:, None, :].astype(x.dtype), window_strides=(1,),
        padding=[(DN_CONV // 2, DN_CONV // 2)],
        dimension_numbers=('NWC', 'WIO', 'NWC'), feature_group_count=ch)


def retention_scan(log_gamma, q, k, v, s0):
    b, h, l, _ = q.shape
    dv = v.shape[-1]
    n = l // RET_CHUNK
    idx = jnp.arange(RET_CHUNK, dtype=jnp.float32)
    lg = log_gamma.astype(jnp.float32)[:, None]
    rel = idx[:, None] - idx[None, :]
    intra = jnp.where(rel >= 0, jnp.exp(jnp.maximum(rel, 0.0) * lg[:, :, None]), 0.0)
    q_dec = jnp.exp((idx + 1.0) * lg)[..., None]
    k_dec = jnp.exp((RET_CHUNK - 1.0 - idx) * lg)[..., None]
    c_dec = jnp.exp(RET_CHUNK * lg)[:, :, None]
    to_chunks = lambda t: jnp.moveaxis(t.reshape(b, h, n, RET_CHUNK, t.shape[-1]), 2, 0)

    def step(s, inp):
        qc, kc, vc = inp
        scores = jnp.einsum('bhid,bhjd->bhij', qc, kc) * intra
        o = (jnp.einsum('bhij,bhje->bhie', scores, vc)
             + jnp.einsum('bhid,bhde->bhie', qc * q_dec, s))
        s = s * c_dec + jnp.einsum('bhjd,bhje->bhde', kc * k_dec, vc)
        return s, o

    s, o = lax.scan(step, s0, (to_chunks(q), to_chunks(k), to_chunks(v)))
    return jnp.moveaxis(o, 0, 2).reshape(b, h, l, dv), s


def gated_delta_scan(q, k, v, g, beta, s0):
    b, h, l, _ = q.shape
    dv = v.shape[-1]
    c = DN_CHUNK
    n = l // c
    ch = lambda t: t.reshape((b, h, n, c) + t.shape[3:])
    q, k, v, beta = ch(q), ch(k), ch(v), ch(beta)
    gc = jnp.cumsum(ch(g), axis=-1)
    idx = jnp.arange(c)
    lower = idx[:, None] >= idx[None, :]
    strict = idx[:, None] > idx[None, :]
    diff = gc[..., :, None] - gc[..., None, :]
    decay = jnp.where(lower, jnp.exp(jnp.where(lower, diff, 0.0)), 0.0)
    k_beta = k * beta[..., None]
    kk = jnp.einsum('bhnid,bhnjd->bhnij', k_beta, k) * decay
    eye = jnp.eye(c, dtype=jnp.float32)
    a = eye + jnp.where(strict, kk, 0.0)
    t_inv = lax.linalg.triangular_solve(a, jnp.broadcast_to(eye, a.shape), left_side=True,
                                        lower=True, unit_diagonal=True)
    u = jnp.einsum('bhnij,bhnjd->bhnid', t_inv, v * beta[..., None])
    w = jnp.einsum('bhnij,bhnjd->bhnid', t_inv, k_beta * jnp.exp(gc)[..., None])
    qk = jnp.where(lower, jnp.einsum('bhnid,bhnjd->bhnij', q, k) * decay, 0.0)
    q_in = q * jnp.exp(gc)[..., None]
    k_out = k * jnp.exp(gc[..., -1:] - gc)[..., None]
    g_last = jnp.exp(gc[..., -1])
    mv = lambda t: jnp.moveaxis(t, 2, 0)

    def step(s, inp):
        u_c, w_c, qk_c, qin_c, kout_c, gl_c = inp
        v_new = u_c - jnp.einsum('bhid,bhde->bhie', w_c, s)
        o = (jnp.einsum('bhid,bhde->bhie', qin_c, s)
             + jnp.einsum('bhij,bhje->bhie', qk_c, v_new))
        s = s * gl_c[..., None, None] + jnp.einsum('bhjd,bhje->bhde', kout_c, v_new)
        return s, o

    s, o = lax.scan(step, s0, tuple(mv(t) for t in (u, w, qk, q_in, k_out, g_last)))
    return jnp.moveaxis(o, 0, 2).reshape(b, h, l, dv), s


def bidir_with_prefix(scan_f, scan_b, lat_f, lat_b, ctx_f, ctx_b, s0):
    flip = lambda t: jnp.flip(t, axis=2)
    oc_f, sc_f = scan_f(*ctx_f, s0)
    oc_b, sc_b = scan_b(*[flip(t) for t in ctx_b], s0)
    ox_f, _ = scan_f(*lat_f, sc_f)
    ox_b, _ = scan_b(*[flip(t) for t in lat_b], sc_b)
    return ox_f + flip(ox_b), oc_f + flip(oc_b)


def retention_mixer(hx, hc, w_in, log_decay, w_out, cos, sin):
    def project(h, rotate):
        q, k, v, g = jnp.split(h @ w_in, [RET_QK, 2 * RET_QK, 2 * RET_QK + RET_VW], axis=-1)
        q = _heads(q, RET_HEADS, RET_DK).astype(jnp.float32)
        k = _heads(k, RET_HEADS, RET_DK).astype(jnp.float32) * RET_DK ** -0.5
        v = _heads(v, RET_HEADS, RET_DV).astype(jnp.float32)
        if rotate:
            q, k = apply_rotary(q, cos, sin), apply_rotary(k, cos, sin)
        return (q, k, v), g

    lat, gx = project(hx, True)
    ctx_in, gc = project(hc, False)
    scan_f = functools.partial(retention_scan, log_decay[0])
    scan_b = functools.partial(retention_scan, log_decay[1])
    s0 = jnp.zeros((hx.shape[0], RET_HEADS, RET_DK, RET_DV), jnp.float32)
    ox, oc = bidir_with_prefix(scan_f, scan_b, lat, lat, ctx_in, ctx_in, s0)

    def finish(o, g, dtype):
        y = _merge_heads(_standardize(o)) * jax.nn.silu(g.astype(jnp.float32))
        return y.astype(dtype) @ w_out

    return finish(ox, gx, hx.dtype), finish(oc, gc, hc.dtype)


def deltanet_mixer(hx, hc, w_in, conv_w, a_log, dt_bias, norm_w, w_out):
    rep = DN_V_HEADS // DN_K_HEADS

    def gates(b_raw, a_raw, d):
        beta = jax.nn.sigmoid(b_raw)
        g = (-jnp.exp(a_log[d].astype(jnp.float32))[:, None]
             * jax.nn.softplus(a_raw + dt_bias[d].astype(jnp.float32)[:, None]))
        return g, beta

    def project(h):
        qkv, z, ba = jnp.split(h @ w_in, [DN_CONV_CH, DN_CONV_CH + DN_VW], axis=-1)
        qkv = jax.nn.silu(short_conv(qkv, conv_w))
        q, k, v = jnp.split(qkv, [DN_QK, 2 * DN_QK], axis=-1)
        q = jnp.repeat(_l2norm(_heads(q, DN_K_HEADS, DN_DK)), rep, axis=1) * DN_DK ** -0.5
        k = jnp.repeat(_l2norm(_heads(k, DN_K_HEADS, DN_DK)), rep, axis=1)
        v = _heads(v, DN_V_HEADS, DN_DV).astype(jnp.float32)
        ba = ba.astype(jnp.float32).transpose(0, 2, 1)
        b_f, a_f, b_b, a_b = jnp.split(ba, 4, axis=1)
        g_f, be_f = gates(b_f, a_f, 0)
        g_b, be_b = gates(b_b, a_b, 1)
        return (q, k, v, g_f, be_f), (q, k, v, g_b, be_b), z

    lat_f, lat_b, zx = project(hx)
    ctx_f, ctx_b, zc = project(hc)
    s0 = jnp.zeros((hx.shape[0], DN_V_HEADS, DN_DK, DN_DV), jnp.float32)
    ox, oc = bidir_with_prefix(gated_delta_scan, gated_delta_scan, lat_f, lat_b, ctx_f, ctx_b, s0)

    def finish(o, z, dtype):
        y = _merge_heads(_rms(o) * norm_w.astype(jnp.float32)) * jax.nn.silu(z.astype(jnp.float32))
        return y.astype(dtype) @ w_out

    return finish(ox, zx, hx.dtype), finish(oc, zc, hc.dtype)


def setup_inputs(seed: int = 0) -> dict:
    key = jax.random.key(seed)
    ks = jax.random.split(key, 24)
    f32 = jnp.float32
    d = D_MODEL
    n_ret = (DEPTH + N_MIXERS - 1) // N_MIXERS
    n_dn = DEPTH // N_MIXERS
    nrm = lambda k, shape, scale: jax.random.normal(k, shape, f32) * scale
    ret_base = jnp.log(1.0 - 2.0 ** (-5.0 - jnp.arange(RET_HEADS, dtype=f32)))
    dt = jnp.exp(jax.random.uniform(ks[17], (n_dn, 2, DN_V_HEADS), f32,
                                    float(np.log(1e-3)), float(np.log(1e-1))))
    return {
        "x": nrm(ks[0], (BATCH, SEQ, d), 1.0),
        "c": nrm(ks[1], (BATCH, d), 1.0),
        "ctx": nrm(ks[2], (BATCH, CTX_LEN, d), 1.0),
        "c_ctx": nrm(ks[3], (d,), 1.0),
        "mod_w": nrm(ks[4], (DEPTH, d, N_MOD * d), d ** -0.5),
        "mod_b": nrm(ks[5], (DEPTH, N_MOD * d), 0.02),
        "ln_g": 1.0 + nrm(ks[6], (DEPTH, 3, d), 0.02),
        "ln_b": nrm(ks[7], (DEPTH, 3, d), 0.02),
        "ffn_w_in": nrm(ks[8], (DEPTH, 2, d, 2 * FFN_HIDDEN), d ** -0.5),
        "ffn_w_out": nrm(ks[9], (DEPTH, 2, FFN_HIDDEN, d), BETA * FFN_HIDDEN ** -0.5),
        "ret_w_in": nrm(ks[10], (n_ret, d, RET_IN), d ** -0.5),
        "ret_log_decay": ret_base * jnp.exp(nrm(ks[11], (n_ret, 2, RET_HEADS), 0.1)),
        "ret_w_out": nrm(ks[12], (n_ret, RET_VW, d), BETA * RET_VW ** -0.5),
        "dn_w_in": nrm(ks[13], (n_dn, d, DN_IN), d ** -0.5),
        "dn_conv_w": nrm(ks[14], (n_dn, DN_CONV, DN_CONV_CH), DN_CONV ** -0.5),
        "dn_a_log": jnp.log(jax.random.uniform(ks[15], (n_dn, 2, DN_V_HEADS), f32, 1.0, 16.0)),
        "dn_dt_bias": dt + jnp.log(-jnp.expm1(-dt)),
        "dn_norm_w": 1.0 + nrm(ks[16], (n_dn, DN_DV), 0.02),
        "dn_w_out": nrm(ks[18], (n_dn, DN_VW, d), BETA * DN_VW ** -0.5),
    }


def reference(x, c, ctx, c_ctx, mod_w, mod_b, ln_g, ln_b, ffn_w_in, ffn_w_out,
              ret_w_in, ret_log_decay, ret_w_out,
              dn_w_in, dn_conv_w, dn_a_log, dn_dt_bias, dn_norm_w, dn_w_out):
    cos, sin = axial_rotary(x.shape[1])
    for i in range(DEPTH):
        j = i // N_MIXERS
        m_x = ada_modulation(c, mod_w[i], mod_b[i])
        m_c = ada_modulation(c_ctx, mod_w[i], mod_b[i])
        x = ffn_step(x, m_x, 0, ffn_w_in[i, 0], ffn_w_out[i, 0], ln_g[i, 0], ln_b[i, 0])
        ctx = ffn_step(ctx, m_c, 0, ffn_w_in[i, 0], ffn_w_out[i, 0], ln_g[i, 0], ln_b[i, 0])
        sx, scx, gx = mod_terms(m_x, 1)
        sc, scc, gc = mod_terms(m_c, 1)
        hx = x * (1 + scx) + sx
        hc = ctx * (1 + scc) + sc
        if i % N_MIXERS == 0:
            ox, oc = retention_mixer(hx, hc, ret_w_in[j], ret_log_decay[j], ret_w_out[j], cos, sin)
        else:
            ox, oc = deltanet_mixer(hx, hc, dn_w_in[j], dn_conv_w[j], dn_a_log[j], dn_dt_bias[j],
                                    dn_norm_w[j], dn_w_out[j])
        x = layer_norm(ALPHA * x + gx * ox, ln_g[i, 1], ln_b[i, 1])
        x = ffn_step(x, m_x, 2, ffn_w_in[i, 1], ffn_w_out[i, 1], ln_g[i, 2], ln_b[i, 2])
        if i < DEPTH - 1:
            ctx = layer_norm(ALPHA * ctx + gc * oc, ln_g[i, 1], ln_b[i, 1])
            ctx = ffn_step(ctx, m_c, 2, ffn_w_in[i, 1], ffn_w_out[i, 1], ln_g[i, 2], ln_b[i, 2])
    return x
```

```python
import functools

import jax
import jax.numpy as jnp
from jax import lax
from jax.experimental import pallas as pl
from jax.experimental.pallas import tpu as pltpu

D_MODEL = 2048
SEQ = 2048
DEPTH = 4
GRID_W = 64
CTX_LEN = 256
SEG = SEQ + CTX_LEN
N_MIXERS = 2
RET_HEADS = D_MODEL // 256
RET_DK = 256
RET_DV = 2 * RET_DK
RET_CHUNK = 128
RET_QK = RET_HEADS * RET_DK
RET_VW = RET_HEADS * RET_DV
RET_IN = 2 * RET_QK + 2 * RET_VW
ROPE_BASE = 10000.0
DN_K_HEADS = D_MODEL // 128
DN_V_HEADS = 2 * DN_K_HEADS
DN_DK = 128
DN_DV = 128
DN_CHUNK = 64
DN_CONV = 5
DN_QK = DN_K_HEADS * DN_DK
DN_VW = DN_V_HEADS * DN_DV
DN_CONV_CH = 2 * DN_QK + DN_VW
DN_IN = DN_CONV_CH + DN_VW + 4 * DN_V_HEADS
FFN_HIDDEN = 5504
N_MOD = 9
ALPHA = (2 * DEPTH) ** 0.25
LN_EPS = 1e-5
NORM_EPS = 1e-6

F32 = jnp.float32
BF16 = jnp.bfloat16

VMEM_LIMIT_BYTES = 56 * 1024 * 1024
LANES = 128

FFN_TH = 512
FFN_HP = -(-FFN_HIDDEN // FFN_TH) * FFN_TH
FFN_TM = 576
PROJ_TM = 1152
PROJ_TN = 512
DN_IN_PAD = -(-DN_IN // PROJ_TN) * PROJ_TN
OUT_TM = 576
OUT_TK = 1024
MOD_TN = 1024

RET_NCHUNK = SEG // RET_CHUNK
RET_LAT_CHUNKS = SEQ // RET_CHUNK
DN_NCHUNK = SEG // DN_CHUNK
DN_LAT_CHUNKS = SEQ // DN_CHUNK
GATE_GROUP = 256


def _params(n_axes):
    return pltpu.CompilerParams(
        dimension_semantics=("arbitrary",) * n_axes,
        vmem_limit_bytes=VMEM_LIMIT_BYTES)


def _dot(a, b):
    return jnp.dot(a.astype(BF16), b.astype(BF16), preferred_element_type=F32)


def _dot_nt(a, b):
    return lax.dot_general(a.astype(BF16), b.astype(BF16), (((1,), (1,)), ((), ())),
                           preferred_element_type=F32)


def _dot_tn(a, b):
    return lax.dot_general(a.astype(BF16), b.astype(BF16), (((0,), (0,)), ((), ())),
                           preferred_element_type=F32)


def _silu(t):
    return t * jax.nn.sigmoid(t)


def _rows_are_ctx(block, tm):
    blocks_per_seg = SEG // tm
    r0 = (block % blocks_per_seg) * tm
    return r0 + lax.broadcasted_iota(jnp.int32, (tm, 1), 0) >= SEQ


def _pick_mod(is_ctx, mx_ref, mc_ref, r):
    return jnp.where(is_ctx, mc_ref[0, r:r + 1, :], mx_ref[0, r:r + 1, :])


def _layer_norm(z, g, b):
    mu = jnp.mean(z, axis=-1, keepdims=True)
    zc = z - mu
    var = jnp.mean(zc * zc, axis=-1, keepdims=True)
    return zc * lax.rsqrt(var + LN_EPS) * g + b


def _mod_kernel(c_ref, w_ref, b_ref, o_ref):
    a = _silu(c_ref[...])
    o_ref[0] = _dot(a, w_ref[0]) + b_ref[0]


def _modulation(cond, mod_w, mod_b):
    n = N_MOD * D_MODEL
    rows = cond.shape[0]
    return pl.pallas_call(
        _mod_kernel,
        grid=(DEPTH, n // MOD_TN),
        in_specs=[
            pl.BlockSpec((rows, D_MODEL), lambda l, j: (0, 0)),
            pl.BlockSpec((1, D_MODEL, MOD_TN), lambda l, j: (l, 0, j)),
            pl.BlockSpec((1, 1, MOD_TN), lambda l, j: (l, 0, j)),
        ],
        out_specs=pl.BlockSpec((1, rows, MOD_TN), lambda l, j: (l, 0, j)),
        out_shape=jax.ShapeDtypeStruct((DEPTH, rows, n), F32),
        compiler_params=_params(2),
        name="ada_modulation",
    )(cond, mod_w, mod_b.reshape(DEPTH, 1, n))


def _ffn_kernel(x_ref, mx_ref, mc_ref, wg_ref, wu_ref, wo_ref, g_ref, b_ref, o_ref, h_ref, acc_ref,
                *, sub):
    i = pl.program_id(0)
    k = pl.program_id(1)
    is_ctx = _rows_are_ctx(i, FFN_TM)

    @pl.when(k == 0)
    def _():
        shift = _pick_mod(is_ctx, mx_ref, mc_ref, 3 * sub)
        scale = _pick_mod(is_ctx, mx_ref, mc_ref, 3 * sub + 1)
        h_ref[...] = (x_ref[...] * (1 + scale) + shift).astype(BF16)
        acc_ref[...] = jnp.zeros_like(acc_ref)

    h = h_ref[...]
    gate = jnp.dot(h, wg_ref[...], preferred_element_type=F32)
    up = jnp.dot(h, wu_ref[...], preferred_element_type=F32)
    act = _silu(gate) * up
    acc_ref[...] += jnp.dot(act.astype(BF16), wo_ref[...], preferred_element_type=F32)

    @pl.when(k == pl.num_programs(1) - 1)
    def _():
        res_gate = _pick_mod(is_ctx, mx_ref, mc_ref, 3 * sub + 2)
        z = ALPHA * x_ref[...] + 0.5 * res_gate * acc_ref[...]
        o_ref[...] = _layer_norm(z, g_ref[...], b_ref[...])


def _ffn_step(xs, mod, sub, weights, g, b):
    wg, wu, wo = weights
    t = xs.shape[0]
    blocks_per_seg = SEG // FFN_TM
    n_ctx_row = mod.shape[0] - 1
    return pl.pallas_call(
        functools.partial(_ffn_kernel, sub=sub),
        grid=(t // FFN_TM, FFN_HP // FFN_TH),
        in_specs=[
            pl.BlockSpec((FFN_TM, D_MODEL), lambda i, k: (i, 0)),
            pl.BlockSpec((1, N_MOD, D_MODEL), lambda i, k: (i // blocks_per_seg, 0, 0)),
            pl.BlockSpec((1, N_MOD, D_MODEL), lambda i, k: (n_ctx_row, 0, 0)),
            pl.BlockSpec((D_MODEL, FFN_TH), lambda i, k: (0, k)),
            pl.BlockSpec((D_MODEL, FFN_TH), lambda i, k: (0, k)),
            pl.BlockSpec((FFN_TH, D_MODEL), lambda i, k: (k, 0)),
            pl.BlockSpec((1, D_MODEL), lambda i, k: (0, 0)),
            pl.BlockSpec((1, D_MODEL), lambda i, k: (0, 0)),
        ],
        out_specs=pl.BlockSpec((FFN_TM, D_MODEL), lambda i, k: (i, 0)),
        out_shape=jax.ShapeDtypeStruct((t, D_MODEL), F32),
        scratch_shapes=[pltpu.VMEM((FFN_TM, D_MODEL), BF16),
                        pltpu.VMEM((FFN_TM, D_MODEL), F32)],
        compiler_params=_params(2),
        name="ffn_step",
    )(xs, mod, mod, wg, wu, wo, g.reshape(1, D_MODEL), b.reshape(1, D_MODEL))


def _prep_ffn_weights(w_in, w_out):
    pad = FFN_HP - FFN_HIDDEN
    wg = jnp.pad(w_in[:, :FFN_HIDDEN].astype(BF16), ((0, 0), (0, pad)))
    wu = jnp.pad(w_in[:, FFN_HIDDEN:].astype(BF16), ((0, 0), (0, pad)))
    wo = jnp.pad(w_out.astype(BF16), ((0, pad), (0, 0)))
    return wg, wu, wo


def _proj_kernel(x_ref, mx_ref, mc_ref, w_ref, o_ref, h_ref):
    i = pl.program_id(0)

    @pl.when(pl.program_id(1) == 0)
    def _():
        is_ctx = _rows_are_ctx(i, PROJ_TM)
        shift = _pick_mod(is_ctx, mx_ref, mc_ref, 3)
        scale = _pick_mod(is_ctx, mx_ref, mc_ref, 4)
        h_ref[...] = (x_ref[...] * (1 + scale) + shift).astype(BF16)

    o_ref[...] = jnp.dot(h_ref[...], w_ref[...], preferred_element_type=F32)


def _mixer_proj(xs, mod, w):
    t = xs.shape[0]
    n = w.shape[1]
    blocks_per_seg = SEG // PROJ_TM
    n_ctx_row = mod.shape[0] - 1
    return pl.pallas_call(
        _proj_kernel,
        grid=(t // PROJ_TM, n // PROJ_TN),
        in_specs=[
            pl.BlockSpec((PROJ_TM, D_MODEL), lambda i, j: (i, 0)),
            pl.BlockSpec((1, N_MOD, D_MODEL), lambda i, j: (i // blocks_per_seg, 0, 0)),
            pl.BlockSpec((1, N_MOD, D_MODEL), lambda i, j: (n_ctx_row, 0, 0)),
            pl.BlockSpec((D_MODEL, PROJ_TN), lambda i, j: (0, j)),
        ],
        out_specs=pl.BlockSpec((PROJ_TM, PROJ_TN), lambda i, j: (i, j)),
        out_shape=jax.ShapeDtypeStruct((t, n), F32),
        scratch_shapes=[pltpu.VMEM((PROJ_TM, D_MODEL), BF16)],
        compiler_params=_params(2),
        name="mixer_proj",
    )(xs, mod, mod, w)


def _out_kernel(y_ref, w_ref, x_ref, mx_ref, mc_ref, g_ref, b_ref, o_ref, acc_ref):
    i = pl.program_id(0)
    k = pl.program_id(1)

    @pl.when(k == 0)
    def _():
        acc_ref[...] = jnp.zeros_like(acc_ref)

    acc_ref[...] += jnp.dot(y_ref[...], w_ref[...], preferred_element_type=F32)

    @pl.when(k == pl.num_programs(1) - 1)
    def _():
        is_ctx = _rows_are_ctx(i, OUT_TM)
        gate = _pick_mod(is_ctx, mx_ref, mc_ref, 5)
        z = ALPHA * x_ref[...] + gate * acc_ref[...]
        o_ref[...] = _layer_norm(z, g_ref[...], b_ref[...])


def _mixer_out(y, w, xs, mod, g, b):
    t = xs.shape[0]
    kdim = w.shape[0]
    blocks_per_seg = SEG // OUT_TM
    n_ctx_row = mod.shape[0] - 1
    return pl.pallas_call(
        _out_kernel,
        grid=(t // OUT_TM, kdim // OUT_TK),
        in_specs=[
            pl.BlockSpec((OUT_TM, OUT_TK), lambda i, k: (i, k)),
            pl.BlockSpec((OUT_TK, D_MODEL), lambda i, k: (k, 0)),
            pl.BlockSpec((OUT_TM, D_MODEL), lambda i, k: (i, 0)),
            pl.BlockSpec((1, N_MOD, D_MODEL), lambda i, k: (i // blocks_per_seg, 0, 0)),
            pl.BlockSpec((1, N_MOD, D_MODEL), lambda i, k: (n_ctx_row, 0, 0)),
            pl.BlockSpec((1, D_MODEL), lambda i, k: (0, 0)),
            pl.BlockSpec((1, D_MODEL), lambda i, k: (0, 0)),
        ],
        out_specs=pl.BlockSpec((OUT_TM, D_MODEL), lambda i, k: (i, 0)),
        out_shape=jax.ShapeDtypeStruct((t, D_MODEL), F32),
        scratch_shapes=[pltpu.VMEM((OUT_TM, D_MODEL), F32)],
        compiler_params=_params(2),
        name="mixer_out",
    )(y, w, xs, mod, mod, g.reshape(1, D_MODEL), b.reshape(1, D_MODEL))


def _retention_kernel(ld_ref, q_ref, k_ref, v_ref, g_ref, cos_ref, sin_ref, y_ref,
                      qs_ref, ks_ref, o_ref, sf_ref, sb_ref):
    h = pl.program_id(1)
    lg_f = ld_ref[0, h]
    lg_b = ld_ref[1, h]
    half = RET_DK // 2
    c = RET_CHUNK

    def rows_of(chunk):
        return pl.ds(pl.multiple_of(chunk * c, c), c)

    def rotate(n, carry):
        r = rows_of(n)
        cos = cos_ref[r, :]
        sin = sin_ref[r, :]
        for src, dst, scale in ((q_ref, qs_ref, 1.0), (k_ref, ks_ref, RET_DK ** -0.5)):
            t1 = src[r, 0:half] * scale
            t2 = src[r, half:RET_DK] * scale
            dst[r, 0:half] = t1 * cos - t2 * sin
            dst[r, half:RET_DK] = t1 * sin + t2 * cos
        return carry

    lax.fori_loop(0, RET_LAT_CHUNKS, rotate, 0)
    qs_ref[SEQ:SEG, :] = q_ref[SEQ:SEG, :]
    ks_ref[SEQ:SEG, :] = k_ref[SEQ:SEG, :] * RET_DK ** -0.5

    ri = lax.broadcasted_iota(jnp.int32, (c, c), 0)
    ci = lax.broadcasted_iota(jnp.int32, (c, c), 1)
    rel = (ri - ci).astype(F32)
    intra = jnp.where(ri >= ci, jnp.exp(jnp.maximum(rel, 0.0) * lg_f), 0.0) \
        + jnp.where(ri <= ci, jnp.exp(jnp.maximum(-rel, 0.0) * lg_b), 0.0)
    idx = lax.broadcasted_iota(jnp.int32, (c, 1), 0).astype(F32)
    qdec_f = jnp.exp((idx + 1.0) * lg_f)
    kdec_f = jnp.exp((c - 1.0 - idx) * lg_f)
    qdec_b = jnp.exp((c - idx) * lg_b)
    kdec_b = jnp.exp(idx * lg_b)
    cdec_f = jnp.exp(c * lg_f)
    cdec_b = jnp.exp(c * lg_b)

    sf_ref[...] = jnp.zeros_like(sf_ref)
    sb_ref[...] = jnp.zeros_like(sb_ref)

    def body(n, carry):
        n_ctx = RET_NCHUNK - RET_LAT_CHUNKS
        cf = jnp.where(n < n_ctx, RET_LAT_CHUNKS + n, n - n_ctx)
        rf = rows_of(cf)
        qc = qs_ref[rf, :]
        kc = ks_ref[rf, :]
        vc = v_ref[rf, :]
        scores = _dot_nt(qc, kc) * intra
        sf = sf_ref[...]
        o_ref[rf, :] = _dot(scores, vc) + _dot(qc * qdec_f, sf)
        sf_ref[...] = sf * cdec_f + _dot_tn(kc * kdec_f, vc)
        return carry

    def body_b(n, carry):
        cb = RET_NCHUNK - 1 - n
        rb = rows_of(cb)
        qc = qs_ref[rb, :]
        kc = ks_ref[rb, :]
        vc = v_ref[rb, :]
        sb = sb_ref[...]
        o_ref[rb, :] += _dot(qc * qdec_b, sb)
        sb_ref[...] = sb * cdec_b + _dot_tn(kc * kdec_b, vc)
        return carry

    lax.fori_loop(0, RET_NCHUNK, body, 0)
    lax.fori_loop(0, RET_NCHUNK, body_b, 0)

    def finish(n, carry):
        r = rows_of(n)
        o = o_ref[r, :]
        mu = jnp.mean(o, axis=-1, keepdims=True)
        oc = o - mu
        var = jnp.mean(oc * oc, axis=-1, keepdims=True)
        y_ref[r, :] = (oc * lax.rsqrt(var + LN_EPS) * _silu(g_ref[r, :])).astype(BF16)
        return carry

    lax.fori_loop(0, RET_NCHUNK, finish, 0)


def _retention(p, log_decay, cos, sin, batch):
    kb = RET_QK // RET_DK
    vb = 2 * RET_QK // RET_DV
    gb = vb + RET_HEADS
    half = RET_DK // 2
    return pl.pallas_call(
        _retention_kernel,
        grid=(batch, RET_HEADS),
        in_specs=[
            pl.BlockSpec(memory_space=pltpu.SMEM),
            pl.BlockSpec((SEG, RET_DK), lambda b, h: (b, h)),
            pl.BlockSpec((SEG, RET_DK), lambda b, h: (b, kb + h)),
            pl.BlockSpec((SEG, RET_DV), lambda b, h: (b, vb + h)),
            pl.BlockSpec((SEG, RET_DV), lambda b, h: (b, gb + h)),
            pl.BlockSpec((SEQ, half), lambda b, h: (0, 0)),
            pl.BlockSpec((SEQ, half), lambda b, h: (0, 0)),
        ],
        out_specs=pl.BlockSpec((SEG, RET_DV), lambda b, h: (b, h)),
        out_shape=jax.ShapeDtypeStruct((batch * SEG, RET_VW), BF16),
        scratch_shapes=[pltpu.VMEM((SEG, RET_DK), F32),
                        pltpu.VMEM((SEG, RET_DK), F32),
                        pltpu.VMEM((SEG, RET_DV), F32),
                        pltpu.VMEM((RET_DK, RET_DV), F32),
                        pltpu.VMEM((RET_DK, RET_DV), F32)],
        compiler_params=_params(2),
        name="retention",
    )(log_decay, p, p, p, p, cos, sin)


def _axial_rotary():
    rows = SEQ // GRID_W
    pos_r = jnp.repeat(jnp.arange(rows), GRID_W).astype(F32)
    pos_c = jnp.tile(jnp.arange(GRID_W), rows).astype(F32)
    half = RET_DK // 2
    inv = ROPE_BASE ** (-jnp.arange(0, half, 2, dtype=F32) / half)
    ang = jnp.concatenate([pos_r[:, None] * inv, pos_c[:, None] * inv], -1)
    return jnp.cos(ang), jnp.sin(ang)


def _prep_ret_w_in(w):
    def deinterleave(m):
        return m.reshape(D_MODEL, RET_HEADS, RET_DK // 2, 2).transpose(0, 1, 3, 2).reshape(D_MODEL, RET_QK)
    wq = deinterleave(w[:, :RET_QK])
    wk = deinterleave(w[:, RET_QK:2 * RET_QK])
    return jnp.concatenate([wq, wk, w[:, 2 * RET_QK:]], axis=1).astype(BF16)


def _gates_kernel(ba_ref, alog_ref, dtb_ref, o_ref, t_ref):
    rows = GATE_GROUP
    lane = lax.broadcasted_iota(jnp.int32, (1, LANES), 1)
    is_beta = (lane // DN_V_HEADS) % 2 == 0
    ri = lax.broadcasted_iota(jnp.int32, (rows, rows), 0)
    ci = lax.broadcasted_iota(jnp.int32, (rows, rows), 1)
    same = (ri // DN_CHUNK) == (ci // DN_CHUNK)
    tri_prefix = jnp.where(same & (ri >= ci), 1.0, 0.0).astype(BF16)
    tri_suffix = jnp.where(same & (ri <= ci), 1.0, 0.0).astype(BF16)
    neg_a = -jnp.exp(alog_ref[...])
    dtb = dtb_ref[...]
    for grp in range(SEG // rows):
        sl = slice(grp * rows, (grp + 1) * rows)
        ba = ba_ref[sl, :]
        beta = jax.nn.sigmoid(ba)
        t = ba + dtb
        g = neg_a * (jnp.maximum(t, 0.0) + jnp.log1p(jnp.exp(-jnp.abs(t))))
        hi = g.astype(BF16)
        r1 = g - hi.astype(F32)
        mid = r1.astype(BF16)
        lo = (r1 - mid.astype(F32)).astype(BF16)
        prefix = sum(jnp.dot(tri_prefix, p, preferred_element_type=F32) for p in (hi, mid, lo))
        suffix = sum(jnp.dot(tri_suffix, p, preferred_element_type=F32) for p in (hi, mid, lo))
        gc = jnp.where(lane < 2 * DN_V_HEADS, prefix, suffix)
        o_ref[sl, :] = jnp.where(is_beta, beta, gc)
        t_ref[sl, :] = prefix + suffix - g


def _dn_gates(p, a_log, dt_bias, batch):
    zeros = jnp.zeros((DN_V_HEADS,), F32)
    alog = jnp.concatenate([zeros, a_log[0], zeros, a_log[1]]).reshape(1, LANES)
    dtb = jnp.concatenate([zeros, dt_bias[0], zeros, dt_bias[1]]).reshape(1, LANES)
    ba_block = (DN_CONV_CH + DN_VW) // LANES
    shape = jax.ShapeDtypeStruct((batch * SEG, LANES), F32)
    return pl.pallas_call(
        _gates_kernel,
        grid=(batch,),
        in_specs=[
            pl.BlockSpec((SEG, LANES), lambda b: (b, ba_block)),
            pl.BlockSpec((1, LANES), lambda b: (0, 0)),
            pl.BlockSpec((1, LANES), lambda b: (0, 0)),
        ],
        out_specs=[pl.BlockSpec((SEG, LANES), lambda b: (b, 0)),
                   pl.BlockSpec((SEG, LANES), lambda b: (b, 0))],
        out_shape=[shape, shape],
        compiler_params=_params(1),
        name="dn_gates",
    )(p, alog, dtb)


def _gate_layouts(gates, totals, batch):
    c = DN_CHUNK
    g = gates.reshape(batch, SEG, 2, 2, DN_K_HEADS, 2)
    gcol = g.transpose(0, 4, 1, 2, 3, 5).reshape(batch, DN_K_HEADS, SEG, 8)
    gr = g.reshape(batch, DN_NCHUNK, c, 2, 2, DN_K_HEADS, 2)
    pair = gr.transpose(0, 5, 1, 3, 4, 6, 2).reshape(batch, DN_K_HEADS, DN_NCHUNK, 2, 2, 2 * c)
    tt = totals.reshape(batch, DN_NCHUNK, c, 2, 2, DN_K_HEADS, 2)[:, :, :, :, 1]
    tot_pair = tt.transpose(0, 4, 1, 3, 5, 2).reshape(batch, DN_K_HEADS, DN_NCHUNK, 2, 1, 2 * c)
    tot_head = jnp.broadcast_to(
        tt[:, :, 0].transpose(0, 3, 1, 2, 4)[..., None], (batch, DN_K_HEADS, DN_NCHUNK, 2, 2, 2 * c))
    pad = jnp.zeros((batch, DN_K_HEADS, DN_NCHUNK, 2, 3, 2 * c), F32)
    grow = jnp.concatenate([pair, tot_pair, tot_head, pad], axis=4)
    return gcol, grow


DN_PAIRS = 2 * DN_NCHUNK
DN_LEVELS = DN_CHUNK.bit_length() - 1
DN_PREP_UNROLL = 2
DN_INV_UNROLL = 8
DN_SOLVE_UNROLL = 4


def _short_conv_silu(x, w):
    n = x.shape[0]
    row = lax.broadcasted_iota(jnp.int32, (n, 1), 0)
    seg_lo = jnp.where(row >= SEQ, SEQ, 0)
    seg_hi = jnp.where(row >= SEQ, SEG, SEQ)
    acc = x * w[DN_CONV // 2:DN_CONV // 2 + 1, :]
    for tap in range(DN_CONV):
        d = tap - DN_CONV // 2
        if d == 0:
            continue
        shifted = pltpu.roll(x, (-d) % n, 0)
        ok = (row + d >= seg_lo) & (row + d < seg_hi)
        acc = acc + jnp.where(ok, shifted, 0.0) * w[tap:tap + 1, :]
    return _silu(acc)


def _l2norm(t):
    return t * lax.rsqrt(jnp.sum(t * t, axis=-1, keepdims=True) + NORM_EPS)


def _block_diag(a, left, right):
    return jnp.concatenate([a * left, a * right], axis=0)


def _deltanet_kernel(q_ref, k_ref, v_ref, z_ref, cwq_ref, cwk_ref, cwv_ref, gcol_ref, grow_ref,
                     nw_ref, y_ref,
                     qn_ref, kn_ref, vs_ref, o_ref, s_ref, l_ref, x_ref, lo_ref, kt_ref, u_ref, w_ref):
    c = DN_CHUNK
    nc = DN_NCHUNK
    dv = DN_DV

    qn_ref[...] = (_l2norm(_short_conv_silu(q_ref[...], cwq_ref[...])) * DN_DK ** -0.5).astype(BF16)
    kn_ref[...] = _l2norm(_short_conv_silu(k_ref[...], cwk_ref[...])).astype(BF16)
    for j in range(2):
        cols = slice(j * dv, (j + 1) * dv)
        vs_ref[:, cols] = _short_conv_silu(v_ref[:, cols], cwv_ref[:, cols]).astype(BF16)
    o_ref[...] = jnp.zeros_like(o_ref)
    s_ref[...] = jnp.zeros_like(s_ref)

    ri = lax.broadcasted_iota(jnp.int32, (c, 2 * c), 0)
    li = lax.broadcasted_iota(jnp.int32, (c, 2 * c), 1)
    cj = li & (c - 1)
    is_left = li < c
    left = is_left.astype(BF16)
    right = 1 - left
    eye2 = (ri == cj).astype(F32)
    incl = (ri >= cj, ri <= cj)
    strict = (ri > cj, ri < cj)

    def level_mask(lv):
        return ((ri >> (lv + 1)) == (cj >> (lv + 1))) & ((ri >> lv) != (cj >> lv))

    eye_k = (lax.broadcasted_iota(jnp.int32, (2 * c, 2 * c), 0)
             == lax.broadcasted_iota(jnp.int32, (2 * c, 2 * c), 1)).astype(BF16)
    zeros_half = jnp.zeros((c, dv), BF16)

    def rows_of(chunk):
        return pl.ds(pl.multiple_of(chunk * c, c), c)

    def diag2(a_l, a_r):
        return jnp.concatenate([jnp.concatenate([a_l, zeros_half], axis=1),
                                jnp.concatenate([zeros_half, a_r], axis=1)], axis=0)

    def prep(chunk):
        rows = rows_of(chunk)
        kc = kn_ref[rows, :]
        qc = qn_ref[rows, :]
        kk2 = jnp.concatenate([kc, kc], axis=0)
        sq = _dot_nt(jnp.concatenate([kc, qc], axis=0), kk2)
        kt2 = _dot_nt(eye_k, kk2)
        gcol = gcol_ref[rows, :]
        for d in range(2):
            p = d * nc + chunk
            gr = grow_ref[chunk, d]
            beta_col = jnp.where(is_left, gcol[:, 4 * d:4 * d + 1], gcol[:, 4 * d + 1:4 * d + 2])
            gc_col = jnp.where(is_left, gcol[:, 4 * d + 2:4 * d + 3], gcol[:, 4 * d + 3:4 * d + 4])
            gc_row = gr[1:2, :]
            tot_row = gr[2:3, :]
            decay = jnp.where(incl[d], jnp.exp(jnp.where(incl[d], gc_col - gc_row, 0.0)), 0.0)
            lmat = jnp.where(strict[d], sq[0:c, :] * beta_col * decay, 0.0)
            l_ref[p] = lmat.astype(BF16)
            x_ref[p] = eye2 - jnp.where(level_mask(0), lmat, 0.0)
            lo_ref[p] = jnp.concatenate([eye2 * jnp.exp(gc_row), sq[c:2 * c, :] * decay], axis=1).astype(BF16)
            kt_ref[p] = (kt2 * jnp.exp(tot_row - gc_row)).astype(BF16)

    def prep_body(i, carry):
        for u in range(DN_PREP_UNROLL):
            prep(i * DN_PREP_UNROLL + u)
        return carry

    lax.fori_loop(0, nc // DN_PREP_UNROLL, prep_body, 0)

    for lv in range(1, DN_LEVELS):
        lmask = level_mask(lv).astype(BF16)

        def inv_body(i, carry, lmask=lmask):
            for u in range(DN_INV_UNROLL):
                p = i * DN_INV_UNROLL + u
                x = x_ref[p]
                xb = x.astype(BF16)
                y = _dot(l_ref[p] * lmask, _block_diag(xb, left, right))
                z = _dot(xb, _block_diag(y.astype(BF16), left, right))
                x_ref[p] = x - z
            return carry

        lax.fori_loop(0, DN_PAIRS // DN_INV_UNROLL, inv_body, 0)

    def solve_body(i, carry):
        for u in range(DN_SOLVE_UNROLL):
            p = i * DN_SOLVE_UNROLL + u
            d = p // nc
            chunk = p - d * nc
            rows = rows_of(chunk)
            gr = grow_ref[chunk, d]
            beta_row = gr[0:1, :]
            x = x_ref[p]
            tu = x * beta_row
            tw = x * (beta_row * jnp.exp(gr[1:2, :]))
            kc = kn_ref[rows, :]
            u_ref[p] = _dot(tu, diag2(vs_ref[rows, 0:dv], vs_ref[rows, dv:2 * dv]))
            w_ref[p] = _dot(tw, diag2(kc, kc)).astype(BF16)
        return carry

    lax.fori_loop(0, DN_PAIRS // DN_SOLVE_UNROLL, solve_body, 0)

    def scan_body(n, carry):
        n_ctx = nc - DN_LAT_CHUNKS
        chunk_f = jnp.where(n < n_ctx, DN_LAT_CHUNKS + n, n - n_ctx)
        chunk_b = nc - 1 - n
        for d, chunk in ((0, chunk_f), (1, chunk_b)):
            p = d * nc + chunk
            rows = rows_of(chunk)
            qc = qn_ref[rows, :]
            w2 = w_ref[p]
            u2 = u_ref[p]
            gr = grow_ref[chunk, d]
            s_old, qs, vn = [], [], []
            for j in range(2):
                s = s_ref[2 * d + j]
                wq = _dot(jnp.concatenate([w2[:, j * dv:(j + 1) * dv], qc], axis=0), s)
                s_old.append(s)
                vn.append((u2[:, j * dv:(j + 1) * dv] - wq[0:c, :]).astype(BF16))
                qs.append(wq[c:2 * c, :].astype(BF16))
            rhs = jnp.concatenate([diag2(qs[0], qs[1]), diag2(vn[0], vn[1])], axis=0)
            o_ref[rows, :] += _dot(lo_ref[p], rhs)
            ds = _dot(kt_ref[p], diag2(vn[0], vn[1]))
            for j in range(2):
                s_ref[2 * d + j] = s_old[j] * jnp.exp(gr[3 + j:4 + j, :]) + ds[:, j * dv:(j + 1) * dv]
        return carry

    lax.fori_loop(0, nc, scan_body, 0)

    nw = nw_ref[...]
    for j in range(2):
        cols = slice(j * dv, (j + 1) * dv)
        o = o_ref[:, cols]
        rms = o * lax.rsqrt(jnp.mean(o * o, axis=-1, keepdims=True) + NORM_EPS)
        y_ref[:, cols] = (rms * nw * _silu(z_ref[:, cols])).astype(BF16)


def _deltanet(p, conv_w, gcol, grow, norm_w, batch):
    kb = DN_QK // DN_DK
    vb = 2 * DN_QK // (2 * DN_DV)
    zb = DN_CONV_CH // (2 * DN_DV)
    pair = 2 * DN_DV
    c = DN_CHUNK
    return pl.pallas_call(
        _deltanet_kernel,
        grid=(batch, DN_K_HEADS),
        in_specs=[
            pl.BlockSpec((SEG, DN_DK), lambda b, h: (b, h)),
            pl.BlockSpec((SEG, DN_DK), lambda b, h: (b, kb + h)),
            pl.BlockSpec((SEG, pair), lambda b, h: (b, vb + h)),
            pl.BlockSpec((SEG, pair), lambda b, h: (b, zb + h)),
            pl.BlockSpec((DN_CONV, DN_DK), lambda b, h: (0, h)),
            pl.BlockSpec((DN_CONV, DN_DK), lambda b, h: (0, kb + h)),
            pl.BlockSpec((DN_CONV, pair), lambda b, h: (0, vb + h)),
            pl.BlockSpec((None, None, SEG, 8), lambda b, h: (b, h, 0, 0)),
            pl.BlockSpec((None, None, DN_NCHUNK, 2, 8, 2 * c), lambda b, h: (b, h, 0, 0, 0, 0)),
            pl.BlockSpec((1, DN_DV), lambda b, h: (0, 0)),
        ],
        out_specs=pl.BlockSpec((SEG, pair), lambda b, h: (b, h)),
        out_shape=jax.ShapeDtypeStruct((batch * SEG, DN_VW), BF16),
        scratch_shapes=[pltpu.VMEM((SEG, DN_DK), BF16),
                        pltpu.VMEM((SEG, DN_DK), BF16),
                        pltpu.VMEM((SEG, pair), BF16),
                        pltpu.VMEM((SEG, pair), F32),
                        pltpu.VMEM((4, DN_DK, DN_DV), F32),
                        pltpu.VMEM((DN_PAIRS, c, 2 * c), BF16),
                        pltpu.VMEM((DN_PAIRS, c, 2 * c), F32),
                        pltpu.VMEM((DN_PAIRS, c, 4 * c), BF16),
                        pltpu.VMEM((DN_PAIRS, 2 * c, 2 * c), BF16),
                        pltpu.VMEM((DN_PAIRS, c, pair), F32),
                        pltpu.VMEM((DN_PAIRS, c, pair), BF16)],
        compiler_params=_params(2),
        name="deltanet",
    )(p, p, p, p, conv_w, conv_w, conv_w, gcol, grow, norm_w.reshape(1, DN_DV))


def kernel(x, c, ctx, c_ctx, mod_w, mod_b, ln_g, ln_b, ffn_w_in, ffn_w_out, ret_w_in, ret_log_decay,
           ret_w_out, dn_w_in, dn_conv_w, dn_a_log, dn_dt_bias, dn_norm_w, dn_w_out):
    batch = x.shape[0]
    xs = jnp.concatenate([x, ctx], axis=1).reshape(batch * SEG, D_MODEL)
    cond = jnp.concatenate([c, c_ctx[None, :]], axis=0)
    mod = _modulation(cond, mod_w, mod_b).reshape(DEPTH, batch + 1, N_MOD, D_MODEL)
    cos, sin = _axial_rotary()

    for i in range(DEPTH):
        j = i // N_MIXERS
        m = mod[i]
        xs = _ffn_step(xs, m, 0, _prep_ffn_weights(ffn_w_in[i, 0], ffn_w_out[i, 0]), ln_g[i, 0], ln_b[i, 0])
        if i % N_MIXERS == 0:
            p = _mixer_proj(xs, m, _prep_ret_w_in(ret_w_in[j]))
            y = _retention(p, ret_log_decay[j], cos, sin, batch)
            w_out = ret_w_out[j]
        else:
            w_in = jnp.pad(dn_w_in[j].astype(BF16), ((0, 0), (0, DN_IN_PAD - DN_IN)))
            p = _mixer_proj(xs, m, w_in)
            gates, totals = _dn_gates(p, dn_a_log[j], dn_dt_bias[j], batch)
            gcol, grow = _gate_layouts(gates, totals, batch)
            y = _deltanet(p, dn_conv_w[j], gcol, grow, dn_norm_w[j], batch)
            w_out = dn_w_out[j]
        xs = _mixer_out(y, w_out.astype(BF16), xs, m, ln_g[i, 1], ln_b[i, 1])
        xs = _ffn_step(xs, m, 2, _prep_ffn_weights(ffn_w_in[i, 1], ffn_w_out[i, 1]), ln_g[i, 2], ln_b[i, 2])

    return xs.reshape(batch, SEG, D_MODEL)[:, :SEQ]
```

```python
import functools

import jax
import jax.numpy as jnp
from jax import lax
from jax.experimental import pallas as pl
from jax.experimental.pallas import tpu as pltpu

D_MODEL = 2048
SEQ = 2048
DEPTH = 4
GRID_W = 64
CTX_LEN = 256
SEG = SEQ + CTX_LEN
N_MIXERS = 2
RET_HEADS = D_MODEL // 256
RET_DK = 256
RET_DV = 2 * RET_DK
RET_CHUNK = 128
RET_QK = RET_HEADS * RET_DK
RET_VW = RET_HEADS * RET_DV
RET_IN = 2 * RET_QK + 2 * RET_VW
ROPE_BASE = 10000.0
DN_K_HEADS = D_MODEL // 128
DN_V_HEADS = 2 * DN_K_HEADS
DN_DK = 128
DN_DV = 128
DN_CHUNK = 64
DN_CONV = 5
DN_QK = DN_K_HEADS * DN_DK
DN_VW = DN_V_HEADS * DN_DV
DN_CONV_CH = 2 * DN_QK + DN_VW
DN_IN = DN_CONV_CH + DN_VW + 4 * DN_V_HEADS
FFN_HIDDEN = 5504
N_MOD = 9
ALPHA = (2 * DEPTH) ** 0.25
LN_EPS = 1e-5
NORM_EPS = 1e-6

F32 = jnp.float32
BF16 = jnp.bfloat16

VMEM_LIMIT_BYTES = 56 * 1024 * 1024
LANES = 128

FFN_TH = 512
FFN_HP = -(-FFN_HIDDEN // FFN_TH) * FFN_TH
FFN_TM = 576
PROJ_TM = 1152
PROJ_TN = 512
DN_IN_PAD = -(-DN_IN // PROJ_TN) * PROJ_TN
OUT_TM = 576
OUT_TK = 1024
MOD_TN = 1024

RET_NCHUNK = SEG // RET_CHUNK
RET_LAT_CHUNKS = SEQ // RET_CHUNK
DN_NCHUNK = SEG // DN_CHUNK
DN_LAT_CHUNKS = SEQ // DN_CHUNK
GATE_GROUP = 256


def _params(n_axes):
    return pltpu.CompilerParams(
        dimension_semantics=("arbitrary",) * n_axes,
        vmem_limit_bytes=VMEM_LIMIT_BYTES)


def _dot(a, b):
    return jnp.dot(a.astype(BF16), b.astype(BF16), preferred_element_type=F32)


def _dot_nt(a, b):
    return lax.dot_general(a.astype(BF16), b.astype(BF16), (((1,), (1,)), ((), ())),
                           preferred_element_type=F32)


def _dot_tn(a, b):
    return lax.dot_general(a.astype(BF16), b.astype(BF16), (((0,), (0,)), ((), ())),
                           preferred_element_type=F32)


def _silu(t):
    return t * jax.nn.sigmoid(t)


def _rows_are_ctx(block, tm):
    blocks_per_seg = SEG // tm
    r0 = (block % blocks_per_seg) * tm
    return r0 + lax.broadcasted_iota(jnp.int32, (tm, 1), 0) >= SEQ


def _pick_mod(is_ctx, mx_ref, mc_ref, r):
    return jnp.where(is_ctx, mc_ref[0, r:r + 1, :], mx_ref[0, r:r + 1, :])


def _layer_norm(z, g, b):
    mu = jnp.mean(z, axis=-1, keepdims=True)
    zc = z - mu
    var = jnp.mean(zc * zc, axis=-1, keepdims=True)
    return zc * lax.rsqrt(var + LN_EPS) * g + b


def _mod_kernel(c_ref, w_ref, b_ref, o_ref):
    a = _silu(c_ref[...])
    o_ref[0] = _dot(a, w_ref[0]) + b_ref[0]


def _modulation(cond, mod_w, mod_b):
    n = N_MOD * D_MODEL
    rows = cond.shape[0]
    return pl.pallas_call(
        _mod_kernel,
        grid=(DEPTH, n // MOD_TN),
        in_specs=[
            pl.BlockSpec((rows, D_MODEL), lambda l, j: (0, 0)),
            pl.BlockSpec((1, D_MODEL, MOD_TN), lambda l, j: (l, 0, j)),
            pl.BlockSpec((1, 1, MOD_TN), lambda l, j: (l, 0, j)),
        ],
        out_specs=pl.BlockSpec((1, rows, MOD_TN), lambda l, j: (l, 0, j)),
        out_shape=jax.ShapeDtypeStruct((DEPTH, rows, n), F32),
        compiler_params=_params(2),
        name="ada_modulation",
    )(cond, mod_w, mod_b.reshape(DEPTH, 1, n))


def _ffn_kernel(x_ref, mx_ref, mc_ref, wg_ref, wu_ref, wo_ref, g_ref, b_ref, o_ref, h_ref, acc_ref,
                *, sub):
    i = pl.program_id(0)
    k = pl.program_id(1)
    is_ctx = _rows_are_ctx(i, FFN_TM)

    @pl.when(k == 0)
    def _():
        shift = _pick_mod(is_ctx, mx_ref, mc_ref, 3 * sub)
        scale = _pick_mod(is_ctx, mx_ref, mc_ref, 3 * sub + 1)
        h_ref[...] = (x_ref[...] * (1 + scale) + shift).astype(BF16)
        acc_ref[...] = jnp.zeros_like(acc_ref)

    h = h_ref[...]
    gate = jnp.dot(h, wg_ref[...], preferred_element_type=F32)
    up = jnp.dot(h, wu_ref[...], preferred_element_type=F32)
    act = _silu(gate) * up
    acc_ref[...] += jnp.dot(act.astype(BF16), wo_ref[...], preferred_element_type=F32)

    @pl.when(k == pl.num_programs(1) - 1)
    def _():
        res_gate = _pick_mod(is_ctx, mx_ref, mc_ref, 3 * sub + 2)
        z = ALPHA * x_ref[...] + 0.5 * res_gate * acc_ref[...]
        o_ref[...] = _layer_norm(z, g_ref[...], b_ref[...])


def _ffn_step(xs, mod, sub, weights, g, b):
    wg, wu, wo = weights
    t = xs.shape[0]
    blocks_per_seg = SEG // FFN_TM
    n_ctx_row = mod.shape[0] - 1
    return pl.pallas_call(
        functools.partial(_ffn_kernel, sub=sub),
        grid=(t // FFN_TM, FFN_HP // FFN_TH),
        in_specs=[
            pl.BlockSpec((FFN_TM, D_MODEL), lambda i, k: (i, 0)),
            pl.BlockSpec((1, N_MOD, D_MODEL), lambda i, k: (i // blocks_per_seg, 0, 0)),
            pl.BlockSpec((1, N_MOD, D_MODEL), lambda i, k: (n_ctx_row, 0, 0)),
            pl.BlockSpec((D_MODEL, FFN_TH), lambda i, k: (0, k)),
            pl.BlockSpec((D_MODEL, FFN_TH), lambda i, k: (0, k)),
            pl.BlockSpec((FFN_TH, D_MODEL), lambda i, k: (k, 0)),
            pl.BlockSpec((1, D_MODEL), lambda i, k: (0, 0)),
            pl.BlockSpec((1, D_MODEL), lambda i, k: (0, 0)),
        ],
        out_specs=pl.BlockSpec((FFN_TM, D_MODEL), lambda i, k: (i, 0)),
        out_shape=jax.ShapeDtypeStruct((t, D_MODEL), F32),
        scratch_shapes=[pltpu.VMEM((FFN_TM, D_MODEL), BF16),
                        pltpu.VMEM((FFN_TM, D_MODEL), F32)],
        compiler_params=_params(2),
        name="ffn_step",
    )(xs, mod, mod, wg, wu, wo, g.reshape(1, D_MODEL), b.reshape(1, D_MODEL))


def _prep_ffn_weights(w_in, w_out):
    pad = FFN_HP - FFN_HIDDEN
    wg = jnp.pad(w_in[:, :FFN_HIDDEN].astype(BF16), ((0, 0), (0, pad)))
    wu = jnp.pad(w_in[:, FFN_HIDDEN:].astype(BF16), ((0, 0), (0, pad)))
    wo = jnp.pad(w_out.astype(BF16), ((0, pad), (0, 0)))
    return wg, wu, wo


def _proj_kernel(x_ref, mx_ref, mc_ref, w_ref, o_ref, h_ref):
    i = pl.program_id(0)

    @pl.when(pl.program_id(1) == 0)
    def _():
        is_ctx = _rows_are_ctx(i, PROJ_TM)
        shift = _pick_mod(is_ctx, mx_ref, mc_ref, 3)
        scale = _pick_mod(is_ctx, mx_ref, mc_ref, 4)
        h_ref[...] = (x_ref[...] * (1 + scale) + shift).astype(BF16)

    o_ref[...] = jnp.dot(h_ref[...], w_ref[...], preferred_element_type=F32)


def _mixer_proj(xs, mod, w):
    t = xs.shape[0]
    n = w.shape[1]
    blocks_per_seg = SEG // PROJ_TM
    n_ctx_row = mod.shape[0] - 1
    return pl.pallas_call(
        _proj_kernel,
        grid=(t // PROJ_TM, n // PROJ_TN),
        in_specs=[
            pl.BlockSpec((PROJ_TM, D_MODEL), lambda i, j: (i, 0)),
            pl.BlockSpec((1, N_MOD, D_MODEL), lambda i, j: (i // blocks_per_seg, 0, 0)),
            pl.BlockSpec((1, N_MOD, D_MODEL), lambda i, j: (n_ctx_row, 0, 0)),
            pl.BlockSpec((D_MODEL, PROJ_TN), lambda i, j: (0, j)),
        ],
        out_specs=pl.BlockSpec((PROJ_TM, PROJ_TN), lambda i, j: (i, j)),
        out_shape=jax.ShapeDtypeStruct((t, n), F32),
        scratch_shapes=[pltpu.VMEM((PROJ_TM, D_MODEL), BF16)],
        compiler_params=_params(2),
        name="mixer_proj",
    )(xs, mod, mod, w)


def _out_kernel(y_ref, w_ref, x_ref, mx_ref, mc_ref, g_ref, b_ref, o_ref, acc_ref):
    i = pl.program_id(0)
    k = pl.program_id(1)

    @pl.when(k == 0)
    def _():
        acc_ref[...] = jnp.zeros_like(acc_ref)

    acc_ref[...] += jnp.dot(y_ref[...], w_ref[...], preferred_element_type=F32)

    @pl.when(k == pl.num_programs(1) - 1)
    def _():
        is_ctx = _rows_are_ctx(i, OUT_TM)
        gate = _pick_mod(is_ctx, mx_ref, mc_ref, 5)
        z = ALPHA * x_ref[...] + gate * acc_ref[...]
        o_ref[...] = _layer_norm(z, g_ref[...], b_ref[...])


def _mixer_out(y, w, xs, mod, g, b):
    t = xs.shape[0]
    kdim = w.shape[0]
    blocks_per_seg = SEG // OUT_TM
    n_ctx_row = mod.shape[0] - 1
    return pl.pallas_call(
        _out_kernel,
        grid=(t // OUT_TM, kdim // OUT_TK),
        in_specs=[
            pl.BlockSpec((OUT_TM, OUT_TK), lambda i, k: (i, k)),
            pl.BlockSpec((OUT_TK, D_MODEL), lambda i, k: (k, 0)),
            pl.BlockSpec((OUT_TM, D_MODEL), lambda i, k: (i, 0)),
            pl.BlockSpec((1, N_MOD, D_MODEL), lambda i, k: (i // blocks_per_seg, 0, 0)),
            pl.BlockSpec((1, N_MOD, D_MODEL), lambda i, k: (n_ctx_row, 0, 0)),
            pl.BlockSpec((1, D_MODEL), lambda i, k: (0, 0)),
            pl.BlockSpec((1, D_MODEL), lambda i, k: (0, 0)),
        ],
        out_specs=pl.BlockSpec((OUT_TM, D_MODEL), lambda i, k: (i, 0)),
        out_shape=jax.ShapeDtypeStruct((t, D_MODEL), F32),
        scratch_shapes=[pltpu.VMEM((OUT_TM, D_MODEL), F32)],
        compiler_params=_params(2),
        name="mixer_out",
    )(y, w, xs, mod, mod, g.reshape(1, D_MODEL), b.reshape(1, D_MODEL))


def _retention_kernel(ld_ref, q_ref, k_ref, v_ref, g_ref, cos_ref, sin_ref, y_ref,
                      qs_ref, ks_ref, o_ref, sf_ref, sb_ref):
    h = pl.program_id(1)
    lg_f = ld_ref[0, h]
    lg_b = ld_ref[1, h]
    half = RET_DK // 2
    c = RET_CHUNK

    def rows_of(chunk):
        return pl.ds(pl.multiple_of(chunk * c, c), c)

    def rotate(n, carry):
        r = rows_of(n)
        cos = cos_ref[r, :]
        sin = sin_ref[r, :]
        for src, dst, scale in ((q_ref, qs_ref, 1.0), (k_ref, ks_ref, RET_DK ** -0.5)):
            t1 = src[r, 0:half] * scale
            t2 = src[r, half:RET_DK] * scale
            dst[r, 0:half] = t1 * cos - t2 * sin
            dst[r, half:RET_DK] = t1 * sin + t2 * cos
        return carry

    lax.fori_loop(0, RET_LAT_CHUNKS, rotate, 0)
    qs_ref[SEQ:SEG, :] = q_ref[SEQ:SEG, :]
    ks_ref[SEQ:SEG, :] = k_ref[SEQ:SEG, :] * RET_DK ** -0.5

    ri = lax.broadcasted_iota(jnp.int32, (c, c), 0)
    ci = lax.broadcasted_iota(jnp.int32, (c, c), 1)
    rel = (ri - ci).astype(F32)
    intra = jnp.where(ri >= ci, jnp.exp(jnp.maximum(rel, 0.0) * lg_f), 0.0) \
        + jnp.where(ri <= ci, jnp.exp(jnp.maximum(-rel, 0.0) * lg_b), 0.0)
    idx = lax.broadcasted_iota(jnp.int32, (c, 1), 0).astype(F32)
    qdec_f = jnp.exp((idx + 1.0) * lg_f)
    kdec_f = jnp.exp((c - 1.0 - idx) * lg_f)
    qdec_b = jnp.exp((c - idx) * lg_b)
    kdec_b = jnp.exp(idx * lg_b)
    cdec_f = jnp.exp(c * lg_f)
    cdec_b = jnp.exp(c * lg_b)

    sf_ref[...] = jnp.zeros_like(sf_ref)
    sb_ref[...] = jnp.zeros_like(sb_ref)

    o_ref[...] = jnp.zeros_like(o_ref)

    def body(n, carry):
        n_ctx = RET_NCHUNK - RET_LAT_CHUNKS
        cf = jnp.where(n < n_ctx, RET_LAT_CHUNKS + n, n - n_ctx)
        cb = RET_NCHUNK - 1 - n
        rf = rows_of(cf)
        rb = rows_of(cb)
        qf = qs_ref[rf, :]
        kf = ks_ref[rf, :]
        vf = v_ref[rf, :]
        qb = qs_ref[rb, :]
        kb = ks_ref[rb, :]
        vb = v_ref[rb, :]
        sf = sf_ref[...]
        sb = sb_ref[...]
        scores = _dot_nt(qf, kf) * intra
        cross_f = _dot(qf * qdec_f, sf)
        cross_b = _dot(qb * qdec_b, sb)
        kv_f = _dot_tn(kf * kdec_f, vf)
        kv_b = _dot_tn(kb * kdec_b, vb)
        o_ref[rf, :] += _dot(scores, vf) + cross_f
        o_ref[rb, :] += cross_b
        sf_ref[...] = sf * cdec_f + kv_f
        sb_ref[...] = sb * cdec_b + kv_b
        return carry

    lax.fori_loop(0, RET_NCHUNK, body, 0)

    def finish(n, carry):
        r = rows_of(n)
        o = o_ref[r, :]
        mu = jnp.mean(o, axis=-1, keepdims=True)
        oc = o - mu
        var = jnp.mean(oc * oc, axis=-1, keepdims=True)
        y_ref[r, :] = (oc * lax.rsqrt(var + LN_EPS) * _silu(g_ref[r, :])).astype(BF16)
        return carry

    lax.fori_loop(0, RET_NCHUNK, finish, 0)


def _retention(p, log_decay, cos, sin, batch):
    kb = RET_QK // RET_DK
    vb = 2 * RET_QK // RET_DV
    gb = vb + RET_HEADS
    half = RET_DK // 2
    return pl.pallas_call(
        _retention_kernel,
        grid=(batch, RET_HEADS),
        in_specs=[
            pl.BlockSpec(memory_space=pltpu.SMEM),
            pl.BlockSpec((SEG, RET_DK), lambda b, h: (b, h)),
            pl.BlockSpec((SEG, RET_DK), lambda b, h: (b, kb + h)),
            pl.BlockSpec((SEG, RET_DV), lambda b, h: (b, vb + h)),
            pl.BlockSpec((SEG, RET_DV), lambda b, h: (b, gb + h)),
            pl.BlockSpec((SEQ, half), lambda b, h: (0, 0)),
            pl.BlockSpec((SEQ, half), lambda b, h: (0, 0)),
        ],
        out_specs=pl.BlockSpec((SEG, RET_DV), lambda b, h: (b, h)),
        out_shape=jax.ShapeDtypeStruct((batch * SEG, RET_VW), BF16),
        scratch_shapes=[pltpu.VMEM((SEG, RET_DK), F32),
                        pltpu.VMEM((SEG, RET_DK), F32),
                        pltpu.VMEM((SEG, RET_DV), F32),
                        pltpu.VMEM((RET_DK, RET_DV), F32),
                        pltpu.VMEM((RET_DK, RET_DV), F32)],
        compiler_params=_params(2),
        name="retention",
    )(log_decay, p, p, p, p, cos, sin)


def _axial_rotary():
    rows = SEQ // GRID_W
    pos_r = jnp.repeat(jnp.arange(rows), GRID_W).astype(F32)
    pos_c = jnp.tile(jnp.arange(GRID_W), rows).astype(F32)
    half = RET_DK // 2
    inv = ROPE_BASE ** (-jnp.arange(0, half, 2, dtype=F32) / half)
    ang = jnp.concatenate([pos_r[:, None] * inv, pos_c[:, None] * inv], -1)
    return jnp.cos(ang), jnp.sin(ang)


def _prep_ret_w_in(w):
    def deinterleave(m):
        return m.reshape(D_MODEL, RET_HEADS, RET_DK // 2, 2).transpose(0, 1, 3, 2).reshape(D_MODEL, RET_QK)
    wq = deinterleave(w[:, :RET_QK])
    wk = deinterleave(w[:, RET_QK:2 * RET_QK])
    return jnp.concatenate([wq, wk, w[:, 2 * RET_QK:]], axis=1).astype(BF16)


def _gates_kernel(ba_ref, alog_ref, dtb_ref, o_ref, t_ref):
    rows = GATE_GROUP
    lane = lax.broadcasted_iota(jnp.int32, (1, LANES), 1)
    is_beta = (lane // DN_V_HEADS) % 2 == 0
    ri = lax.broadcasted_iota(jnp.int32, (rows, rows), 0)
    ci = lax.broadcasted_iota(jnp.int32, (rows, rows), 1)
    same = (ri // DN_CHUNK) == (ci // DN_CHUNK)
    tri_prefix = jnp.where(same & (ri >= ci), 1.0, 0.0).astype(BF16)
    tri_suffix = jnp.where(same & (ri <= ci), 1.0, 0.0).astype(BF16)
    neg_a = -jnp.exp(alog_ref[...])
    dtb = dtb_ref[...]
    for grp in range(SEG // rows):
        sl = slice(grp * rows, (grp + 1) * rows)
        ba = ba_ref[sl, :]
        beta = jax.nn.sigmoid(ba)
        t = ba + dtb
        g = neg_a * (jnp.maximum(t, 0.0) + jnp.log1p(jnp.exp(-jnp.abs(t))))
        hi = g.astype(BF16)
        r1 = g - hi.astype(F32)
        mid = r1.astype(BF16)
        lo = (r1 - mid.astype(F32)).astype(BF16)
        prefix = sum(jnp.dot(tri_prefix, p, preferred_element_type=F32) for p in (hi, mid, lo))
        suffix = sum(jnp.dot(tri_suffix, p, preferred_element_type=F32) for p in (hi, mid, lo))
        gc = jnp.where(lane < 2 * DN_V_HEADS, prefix, suffix)
        o_ref[sl, :] = jnp.where(is_beta, beta, gc)
        t_ref[sl, :] = prefix + suffix - g


def _dn_gates(p, a_log, dt_bias, batch):
    zeros = jnp.zeros((DN_V_HEADS,), F32)
    alog = jnp.concatenate([zeros, a_log[0], zeros, a_log[1]]).reshape(1, LANES)
    dtb = jnp.concatenate([zeros, dt_bias[0], zeros, dt_bias[1]]).reshape(1, LANES)
    ba_block = (DN_CONV_CH + DN_VW) // LANES
    shape = jax.ShapeDtypeStruct((batch * SEG, LANES), F32)
    return pl.pallas_call(
        _gates_kernel,
        grid=(batch,),
        in_specs=[
            pl.BlockSpec((SEG, LANES), lambda b: (b, ba_block)),
            pl.BlockSpec((1, LANES), lambda b: (0, 0)),
            pl.BlockSpec((1, LANES), lambda b: (0, 0)),
        ],
        out_specs=[pl.BlockSpec((SEG, LANES), lambda b: (b, 0)),
                   pl.BlockSpec((SEG, LANES), lambda b: (b, 0))],
        out_shape=[shape, shape],
        compiler_params=_params(1),
        name="dn_gates",
    )(p, alog, dtb)


def _gate_layouts(gates, totals, batch):
    c = DN_CHUNK
    g = gates.reshape(batch, SEG, 2, 2, DN_K_HEADS, 2)
    gcol = g.transpose(0, 4, 1, 2, 3, 5).reshape(batch, DN_K_HEADS, SEG, 8)
    gr = g.reshape(batch, DN_NCHUNK, c, 2, 2, DN_K_HEADS, 2)
    pair = gr.transpose(0, 5, 1, 3, 4, 6, 2).reshape(batch, DN_K_HEADS, DN_NCHUNK, 2, 2, 2 * c)
    tt = totals.reshape(batch, DN_NCHUNK, c, 2, 2, DN_K_HEADS, 2)[:, :, :, :, 1]
    tot_pair = tt.transpose(0, 4, 1, 3, 5, 2).reshape(batch, DN_K_HEADS, DN_NCHUNK, 2, 1, 2 * c)
    tot_head = jnp.broadcast_to(
        tt[:, :, 0].transpose(0, 3, 1, 2, 4)[..., None], (batch, DN_K_HEADS, DN_NCHUNK, 2, 2, 2 * c))
    pad = jnp.zeros((batch, DN_K_HEADS, DN_NCHUNK, 2, 3, 2 * c), F32)
    grow = jnp.concatenate([pair, tot_pair, tot_head, pad], axis=4)
    return gcol, grow


DN_PAIRS = 2 * DN_NCHUNK
DN_LEVELS = DN_CHUNK.bit_length() - 1
DN_PREP_UNROLL = 6
DN_INV_UNROLL = DN_PAIRS
DN_SOLVE_UNROLL = 12


def _short_conv_silu(x, w):
    n = x.shape[0]
    row = lax.broadcasted_iota(jnp.int32, (n, 1), 0)
    seg_lo = jnp.where(row >= SEQ, SEQ, 0)
    seg_hi = jnp.where(row >= SEQ, SEG, SEQ)
    acc = x * w[DN_CONV // 2:DN_CONV // 2 + 1, :]
    for tap in range(DN_CONV):
        d = tap - DN_CONV // 2
        if d == 0:
            continue
        shifted = pltpu.roll(x, (-d) % n, 0)
        ok = (row + d >= seg_lo) & (row + d < seg_hi)
        acc = acc + jnp.where(ok, shifted, 0.0) * w[tap:tap + 1, :]
    return _silu(acc)


def _l2norm(t):
    return t * lax.rsqrt(jnp.sum(t * t, axis=-1, keepdims=True) + NORM_EPS)


def _block_diag(a, left, right):
    return jnp.concatenate([a * left, a * right], axis=0)


def _deltanet_kernel(q_ref, k_ref, v_ref, z_ref, cwq_ref, cwk_ref, cwv_ref, gcol_ref, grow_ref,
                     nw_ref, y_ref,
                     qn_ref, kn_ref, vs_ref, o_ref, s_ref, l_ref, x_ref, lo_ref, kt_ref, u_ref, w_ref):
    c = DN_CHUNK
    nc = DN_NCHUNK
    dv = DN_DV

    qn_ref[...] = (_l2norm(_short_conv_silu(q_ref[...], cwq_ref[...])) * DN_DK ** -0.5).astype(BF16)
    kn_ref[...] = _l2norm(_short_conv_silu(k_ref[...], cwk_ref[...])).astype(BF16)
    for j in range(2):
        cols = slice(j * dv, (j + 1) * dv)
        vs_ref[:, cols] = _short_conv_silu(v_ref[:, cols], cwv_ref[:, cols]).astype(BF16)
    o_ref[...] = jnp.zeros_like(o_ref)
    s_ref[...] = jnp.zeros_like(s_ref)

    ri = lax.broadcasted_iota(jnp.int32, (c, 2 * c), 0)
    li = lax.broadcasted_iota(jnp.int32, (c, 2 * c), 1)
    cj = li & (c - 1)
    is_left = li < c
    left = is_left.astype(BF16)
    right = 1 - left
    eye2 = (ri == cj).astype(F32)
    incl = (ri >= cj, ri <= cj)
    strict = (ri > cj, ri < cj)

    def level_mask(lv):
        return ((ri >> (lv + 1)) == (cj >> (lv + 1))) & ((ri >> lv) != (cj >> lv))

    eye_k = (lax.broadcasted_iota(jnp.int32, (2 * c, 2 * c), 0)
             == lax.broadcasted_iota(jnp.int32, (2 * c, 2 * c), 1)).astype(BF16)
    zeros_half = jnp.zeros((c, dv), BF16)

    def rows_of(chunk):
        return pl.ds(pl.multiple_of(chunk * c, c), c)

    def diag2(a_l, a_r):
        return jnp.concatenate([jnp.concatenate([a_l, zeros_half], axis=1),
                                jnp.concatenate([zeros_half, a_r], axis=1)], axis=0)

    def prep(chunk):
        rows = rows_of(chunk)
        kc = kn_ref[rows, :]
        qc = qn_ref[rows, :]
        kk2 = jnp.concatenate([kc, kc], axis=0)
        sq = _dot_nt(jnp.concatenate([kc, qc], axis=0), kk2)
        kt2 = _dot_nt(eye_k, kk2)
        gcol = gcol_ref[rows, :]
        for d in range(2):
            p = d * nc + chunk
            gr = grow_ref[chunk, d]
            beta_col = jnp.where(is_left, gcol[:, 4 * d:4 * d + 1], gcol[:, 4 * d + 1:4 * d + 2])
            gc_col = jnp.where(is_left, gcol[:, 4 * d + 2:4 * d + 3], gcol[:, 4 * d + 3:4 * d + 4])
            gc_row = gr[1:2, :]
            tot_row = gr[2:3, :]
            decay = jnp.where(incl[d], jnp.exp(jnp.where(incl[d], gc_col - gc_row, 0.0)), 0.0)
            lmat = jnp.where(strict[d], sq[0:c, :] * beta_col * decay, 0.0)
            l_ref[p] = lmat.astype(BF16)
            x_ref[p] = eye2 - jnp.where(level_mask(0), lmat, 0.0)
            lo_ref[p] = jnp.concatenate([eye2 * jnp.exp(gc_row), sq[c:2 * c, :] * decay], axis=1).astype(BF16)
            kt_ref[p] = (kt2 * jnp.exp(tot_row - gc_row)).astype(BF16)

    def prep_body(i, carry):
        for u in range(DN_PREP_UNROLL):
            prep(i * DN_PREP_UNROLL + u)
        return carry

    lax.fori_loop(0, nc // DN_PREP_UNROLL, prep_body, 0)

    for lv in range(1, DN_LEVELS):
        lmask = level_mask(lv).astype(BF16)

        def inv_body(i, carry, lmask=lmask):
            stage = []
            for u in range(DN_INV_UNROLL):
                p = i * DN_INV_UNROLL + u
                x = x_ref[p]
                xb = x.astype(BF16)
                stage.append((p, x, xb, _dot(l_ref[p] * lmask, _block_diag(xb, left, right))))
            for p, x, xb, y in stage:
                x_ref[p] = x - _dot(xb, _block_diag(y.astype(BF16), left, right))
            return carry

        lax.fori_loop(0, DN_PAIRS // DN_INV_UNROLL, inv_body, 0)

    def solve_body(i, carry):
        for u in range(DN_SOLVE_UNROLL):
            p = i * DN_SOLVE_UNROLL + u
            d = p // nc
            chunk = p - d * nc
            rows = rows_of(chunk)
            gr = grow_ref[chunk, d]
            beta_row = gr[0:1, :]
            x = x_ref[p]
            tu = x * beta_row
            tw = x * (beta_row * jnp.exp(gr[1:2, :]))
            kc = kn_ref[rows, :]
            u_ref[p] = _dot(tu, diag2(vs_ref[rows, 0:dv], vs_ref[rows, dv:2 * dv]))
            w_ref[p] = _dot(tw, diag2(kc, kc)).astype(BF16)
        return carry

    lax.fori_loop(0, DN_PAIRS // DN_SOLVE_UNROLL, solve_body, 0)

    def scan_body(n, carry):
        n_ctx = nc - DN_LAT_CHUNKS
        chunk_f = jnp.where(n < n_ctx, DN_LAT_CHUNKS + n, n - n_ctx)
        chunk_b = nc - 1 - n
        stage = []
        for d, chunk in ((0, chunk_f), (1, chunk_b)):
            p = d * nc + chunk
            qc = qn_ref[rows_of(chunk), :]
            w2 = w_ref[p]
            s_old = [s_ref[2 * d + j] for j in range(2)]
            wq = [_dot(jnp.concatenate([w2[:, j * dv:(j + 1) * dv], qc], axis=0), s_old[j])
                  for j in range(2)]
            stage.append((d, chunk, p, s_old, wq))
        for d, chunk, p, s_old, wq in stage:
            rows = rows_of(chunk)
            u2 = u_ref[p]
            gr = grow_ref[chunk, d]
            vn = [(u2[:, j * dv:(j + 1) * dv] - wq[j][0:c, :]).astype(BF16) for j in range(2)]
            qs = [wq[j][c:2 * c, :].astype(BF16) for j in range(2)]
            rhs = jnp.concatenate([diag2(qs[0], qs[1]), diag2(vn[0], vn[1])], axis=0)
            o_ref[d, rows, :] += _dot(lo_ref[p], rhs)
            ds = _dot(kt_ref[p], diag2(vn[0], vn[1]))
            for j in range(2):
                s_ref[2 * d + j] = s_old[j] * jnp.exp(gr[3 + j:4 + j, :]) + ds[:, j * dv:(j + 1) * dv]
        return carry

    lax.fori_loop(0, nc, scan_body, 0)

    nw = nw_ref[...]
    for j in range(2):
        cols = slice(j * dv, (j + 1) * dv)
        o = o_ref[0, :, cols] + o_ref[1, :, cols]
        rms = o * lax.rsqrt(jnp.mean(o * o, axis=-1, keepdims=True) + NORM_EPS)
        y_ref[:, cols] = (rms * nw * _silu(z_ref[:, cols])).astype(BF16)


def _deltanet(p, conv_w, gcol, grow, norm_w, batch):
    kb = DN_QK // DN_DK
    vb = 2 * DN_QK // (2 * DN_DV)
    zb = DN_CONV_CH // (2 * DN_DV)
    pair = 2 * DN_DV
    c = DN_CHUNK
    return pl.pallas_call(
        _deltanet_kernel,
        grid=(batch, DN_K_HEADS),
        in_specs=[
            pl.BlockSpec((SEG, DN_DK), lambda b, h: (b, h)),
            pl.BlockSpec((SEG, DN_DK), lambda b, h: (b, kb + h)),
            pl.BlockSpec((SEG, pair), lambda b, h: (b, vb + h)),
            pl.BlockSpec((SEG, pair), lambda b, h: (b, zb + h)),
            pl.BlockSpec((DN_CONV, DN_DK), lambda b, h: (0, h)),
            pl.BlockSpec((DN_CONV, DN_DK), lambda b, h: (0, kb + h)),
            pl.BlockSpec((DN_CONV, pair), lambda b, h: (0, vb + h)),
            pl.BlockSpec((None, None, SEG, 8), lambda b, h: (b, h, 0, 0)),
            pl.BlockSpec((None, None, DN_NCHUNK, 2, 8, 2 * c), lambda b, h: (b, h, 0, 0, 0, 0)),
            pl.BlockSpec((1, DN_DV), lambda b, h: (0, 0)),
        ],
        out_specs=pl.BlockSpec((SEG, pair), lambda b, h: (b, h)),
        out_shape=jax.ShapeDtypeStruct((batch * SEG, DN_VW), BF16),
        scratch_shapes=[pltpu.VMEM((SEG, DN_DK), BF16),
                        pltpu.VMEM((SEG, DN_DK), BF16),
                        pltpu.VMEM((SEG, pair), BF16),
                        pltpu.VMEM((2, SEG, pair), F32),
                        pltpu.VMEM((4, DN_DK, DN_DV), F32),
                        pltpu.VMEM((DN_PAIRS, c, 2 * c), BF16),
                        pltpu.VMEM((DN_PAIRS, c, 2 * c), F32),
                        pltpu.VMEM((DN_PAIRS, c, 4 * c), BF16),
                        pltpu.VMEM((DN_PAIRS, 2 * c, 2 * c), BF16),
                        pltpu.VMEM((DN_PAIRS, c, pair), F32),
                        pltpu.VMEM((DN_PAIRS, c, pair), BF16)],
        compiler_params=_params(2),
        name="deltanet",
    )(p, p, p, p, conv_w, conv_w, conv_w, gcol, grow, norm_w.reshape(1, DN_DV))


def kernel(x, c, ctx, c_ctx, mod_w, mod_b, ln_g, ln_b, ffn_w_in, ffn_w_out, ret_w_in, ret_log_decay,
           ret_w_out, dn_w_in, dn_conv_w, dn_a_log, dn_dt_bias, dn_norm_w, dn_w_out):
    batch = x.shape[0]
    xs = jnp.concatenate([x, ctx], axis=1).reshape(batch * SEG, D_MODEL)
    cond = jnp.concatenate([c, c_ctx[None, :]], axis=0)
    mod = _modulation(cond, mod_w, mod_b).reshape(DEPTH, batch + 1, N_MOD, D_MODEL)
    cos, sin = _axial_rotary()

    for i in range(DEPTH):
        j = i // N_MIXERS
        m = mod[i]
        xs = _ffn_step(xs, m, 0, _prep_ffn_weights(ffn_w_in[i, 0], ffn_w_out[i, 0]), ln_g[i, 0], ln_b[i, 0])
        if i % N_MIXERS == 0:
            p = _mixer_proj(xs, m, _prep_ret_w_in(ret_w_in[j]))
            y = _retention(p, ret_log_decay[j], cos, sin, batch)
            w_out = ret_w_out[j]
        else:
            w_in = jnp.pad(dn_w_in[j].astype(BF16), ((0, 0), (0, DN_IN_PAD - DN_IN)))
            p = _mixer_proj(xs, m, w_in)
            gates, totals = _dn_gates(p, dn_a_log[j], dn_dt_bias[j], batch)
            gcol, grow = _gate_layouts(gates, totals, batch)
            y = _deltanet(p, dn_conv_w[j], gcol, grow, dn_norm_w[j], batch)
            w_out = dn_w_out[j]
        xs = _mixer_out(y, w_out.astype(BF16), xs, m, ln_g[i, 1], ln_b[i, 1])
        xs = _ffn_step(xs, m, 2, _prep_ffn_weights(ffn_w_in[i, 1], ffn_w_out[i, 1]), ln_g[i, 2], ln_b[i, 2])

    return xs.reshape(batch, SEG, D_MODEL)[:, :SEQ]
```

```python
import functools

import jax
import jax.numpy as jnp
from jax import lax
from jax.experimental import pallas as pl
from jax.experimental.pallas import tpu as pltpu

D_MODEL = 2048
SEQ = 2048
DEPTH = 4
GRID_W = 64
CTX_LEN = 256
SEG = SEQ + CTX_LEN
N_MIXERS = 2
RET_HEADS = D_MODEL // 256
RET_DK = 256
RET_DV = 2 * RET_DK
RET_CHUNK = 128
RET_QK = RET_HEADS * RET_DK
RET_VW = RET_HEADS * RET_DV
RET_IN = 2 * RET_QK + 2 * RET_VW
ROPE_BASE = 10000.0
DN_K_HEADS = D_MODEL // 128
DN_V_HEADS = 2 * DN_K_HEADS
DN_DK = 128
DN_DV = 128
DN_CHUNK = 64
DN_CONV = 5
DN_QK = DN_K_HEADS * DN_DK
DN_VW = DN_V_HEADS * DN_DV
DN_CONV_CH = 2 * DN_QK + DN_VW
DN_IN = DN_CONV_CH + DN_VW + 4 * DN_V_HEADS
FFN_HIDDEN = 5504
N_MOD = 9
ALPHA = (2 * DEPTH) ** 0.25
LN_EPS = 1e-5
NORM_EPS = 1e-6

F32 = jnp.float32
BF16 = jnp.bfloat16

VMEM_LIMIT_BYTES = 56 * 1024 * 1024
LANES = 128

FFN_TH = 512
FFN_TM = 576
PROJ_TM = 1152
PROJ_TN = 512
OUT_TM = 576
OUT_TK = 1024
MOD_TN = 1024

RET_NCHUNK = SEG // RET_CHUNK
RET_LAT_CHUNKS = SEQ // RET_CHUNK
DN_NCHUNK = SEG // DN_CHUNK
DN_LAT_CHUNKS = SEQ // DN_CHUNK
GATE_GROUP = 256


def _params(n_axes):
    return pltpu.CompilerParams(
        dimension_semantics=("arbitrary",) * n_axes,
        vmem_limit_bytes=VMEM_LIMIT_BYTES)


def _dot(a, b):
    return jnp.dot(a.astype(BF16), b.astype(BF16), preferred_element_type=F32)


def _dot_nt(a, b):
    return lax.dot_general(a.astype(BF16), b.astype(BF16), (((1,), (1,)), ((), ())),
                           preferred_element_type=F32)


def _dot_tn(a, b):
    return lax.dot_general(a.astype(BF16), b.astype(BF16), (((0,), (0,)), ((), ())),
                           preferred_element_type=F32)


def _silu(t):
    return t * jax.nn.sigmoid(t)


def _rows_are_ctx(block, tm):
    blocks_per_seg = SEG // tm
    r0 = (block % blocks_per_seg) * tm
    return r0 + lax.broadcasted_iota(jnp.int32, (tm, 1), 0) >= SEQ


def _pick_mod(is_ctx, mx_ref, mc_ref, r):
    return jnp.where(is_ctx, mc_ref[0, r:r + 1, :], mx_ref[0, r:r + 1, :])


def _layer_norm(z, g, b):
    mu = jnp.mean(z, axis=-1, keepdims=True)
    zc = z - mu
    var = jnp.mean(zc * zc, axis=-1, keepdims=True)
    return zc * lax.rsqrt(var + LN_EPS) * g + b


def _mod_kernel(c_ref, w_ref, b_ref, o_ref):
    a = _silu(c_ref[...])
    o_ref[0] = _dot(a, w_ref[0]) + b_ref[0]


def _modulation(cond, mod_w, mod_b):
    n = N_MOD * D_MODEL
    rows = cond.shape[0]
    return pl.pallas_call(
        _mod_kernel,
        grid=(DEPTH, n // MOD_TN),
        in_specs=[
            pl.BlockSpec((rows, D_MODEL), lambda l, j: (0, 0)),
            pl.BlockSpec((1, D_MODEL, MOD_TN), lambda l, j: (l, 0, j)),
            pl.BlockSpec((1, 1, MOD_TN), lambda l, j: (l, 0, j)),
        ],
        out_specs=pl.BlockSpec((1, rows, MOD_TN), lambda l, j: (l, 0, j)),
        out_shape=jax.ShapeDtypeStruct((DEPTH, rows, n), F32),
        compiler_params=_params(2),
        name="ada_modulation",
    )(cond, mod_w, mod_b.reshape(DEPTH, 1, n))


FFN_STEPS = -(-FFN_HIDDEN // FFN_TH)
FFN_TAIL = FFN_HIDDEN - (FFN_STEPS - 1) * FFN_TH


def _ffn_kernel(x_ref, mx_ref, mc_ref, wg_ref, wu_ref, wo_ref, g_ref, b_ref, o_ref, h_ref, acc_ref, *, sub):
    i = pl.program_id(0)
    k = pl.program_id(1)
    last = pl.num_programs(1) - 1
    is_ctx = _rows_are_ctx(i, FFN_TM)

    @pl.when(k == 0)
    def _():
        shift = _pick_mod(is_ctx, mx_ref, mc_ref, 3 * sub)
        scale = _pick_mod(is_ctx, mx_ref, mc_ref, 3 * sub + 1)
        h_ref[...] = (x_ref[...] * (1 + scale) + shift).astype(BF16)
        acc_ref[...] = jnp.zeros_like(acc_ref)

    def accumulate(wg, wu, wo):
        h = h_ref[...]
        gate = _dot(h, wg)
        up = _dot(h, wu)
        acc_ref[...] += _dot(_silu(gate) * up, wo)

    @pl.when(k < last)
    def _():
        accumulate(wg_ref[...], wu_ref[0, 0], wo_ref[0, 0])

    @pl.when(k == last)
    def _():
        accumulate(wg_ref[:, 0:FFN_TAIL], wu_ref[0, 0, :, FFN_TH - FFN_TAIL:FFN_TH],
                   wo_ref[0, 0, FFN_TH - FFN_TAIL:FFN_TH, :])
        res_gate = _pick_mod(is_ctx, mx_ref, mc_ref, 3 * sub + 2)
        z = ALPHA * x_ref[...] + 0.5 * res_gate * acc_ref[...]
        o_ref[...] = _layer_norm(z, g_ref[...], b_ref[...])


def _ffn_step(xs, mod, layer, half, w_in, w_out, g, b):
    t = xs.shape[0]
    one = pl.Element(1)
    blocks_per_seg = SEG // FFN_TM
    n_ctx_row = mod.shape[0] - 1
    return pl.pallas_call(
        functools.partial(_ffn_kernel, sub=2 * half),
        grid=(t // FFN_TM, FFN_STEPS),
        in_specs=[
            pl.BlockSpec((FFN_TM, D_MODEL), lambda i, k: (i, 0)),
            pl.BlockSpec((1, N_MOD, D_MODEL), lambda i, k: (i // blocks_per_seg, 0, 0)),
            pl.BlockSpec((1, N_MOD, D_MODEL), lambda i, k: (n_ctx_row, 0, 0)),
            pl.BlockSpec((None, None, D_MODEL, FFN_TH), lambda i, k: (layer, half, 0, k)),
            pl.BlockSpec((one, one, pl.Element(D_MODEL), pl.Element(FFN_TH)),
                         lambda i, k: (layer, half, 0, jnp.minimum((FFN_HIDDEN + k * FFN_TH) // LANES,
                                                      (2 * FFN_HIDDEN - FFN_TH) // LANES) * LANES)),
            pl.BlockSpec((one, one, pl.Element(FFN_TH), pl.Element(D_MODEL)),
                         lambda i, k: (layer, half, jnp.minimum(k * FFN_TH // LANES,
                                                                (FFN_HIDDEN - FFN_TH) // LANES) * LANES, 0)),
            pl.BlockSpec((1, D_MODEL), lambda i, k: (0, 0)),
            pl.BlockSpec((1, D_MODEL), lambda i, k: (0, 0)),
        ],
        out_specs=pl.BlockSpec((FFN_TM, D_MODEL), lambda i, k: (i, 0)),
        out_shape=jax.ShapeDtypeStruct((t, D_MODEL), F32),
        scratch_shapes=[pltpu.VMEM((FFN_TM, D_MODEL), BF16),
                        pltpu.VMEM((FFN_TM, D_MODEL), F32)],
        compiler_params=_params(2),
        name="ffn_step",
    )(xs, mod, mod, w_in, w_in, w_out, g.reshape(1, D_MODEL), b.reshape(1, D_MODEL))


def _proj_kernel(x_ref, mx_ref, mc_ref, w_ref, o_ref, h_ref):
    i = pl.program_id(0)

    @pl.when(pl.program_id(1) == 0)
    def _():
        is_ctx = _rows_are_ctx(i, PROJ_TM)
        shift = _pick_mod(is_ctx, mx_ref, mc_ref, 3)
        scale = _pick_mod(is_ctx, mx_ref, mc_ref, 4)
        h_ref[...] = (x_ref[...] * (1 + scale) + shift).astype(BF16)

    o_ref[...] = jnp.dot(h_ref[...], w_ref[...], preferred_element_type=F32)


def _mixer_proj(xs, mod, w, layer, tn, first_block=0, n_blocks=None):
    t = xs.shape[0]
    if n_blocks is None:
        n_blocks = w.shape[2] // tn
    blocks_per_seg = SEG // PROJ_TM
    n_ctx_row = mod.shape[0] - 1
    return pl.pallas_call(
        _proj_kernel,
        grid=(t // PROJ_TM, n_blocks),
        in_specs=[
            pl.BlockSpec((PROJ_TM, D_MODEL), lambda i, j: (i, 0)),
            pl.BlockSpec((1, N_MOD, D_MODEL), lambda i, j: (i // blocks_per_seg, 0, 0)),
            pl.BlockSpec((1, N_MOD, D_MODEL), lambda i, j: (n_ctx_row, 0, 0)),
            pl.BlockSpec((None, D_MODEL, tn), lambda i, j: (layer, 0, first_block + j)),
        ],
        out_specs=pl.BlockSpec((PROJ_TM, tn), lambda i, j: (i, j)),
        out_shape=jax.ShapeDtypeStruct((t, n_blocks * tn), F32),
        scratch_shapes=[pltpu.VMEM((PROJ_TM, D_MODEL), BF16)],
        compiler_params=_params(2),
        name="mixer_proj",
    )(xs, mod, mod, w)


def _out_kernel(y_ref, w_ref, x_ref, mx_ref, mc_ref, g_ref, b_ref, o_ref, acc_ref):
    i = pl.program_id(0)
    k = pl.program_id(1)

    @pl.when(k == 0)
    def _():
        acc_ref[...] = jnp.zeros_like(acc_ref)

    acc_ref[...] += jnp.dot(y_ref[...], w_ref[...], preferred_element_type=F32)

    @pl.when(k == pl.num_programs(1) - 1)
    def _():
        is_ctx = _rows_are_ctx(i, OUT_TM)
        gate = _pick_mod(is_ctx, mx_ref, mc_ref, 5)
        z = ALPHA * x_ref[...] + gate * acc_ref[...]
        o_ref[...] = _layer_norm(z, g_ref[...], b_ref[...])


def _mixer_out(y, w, layer, xs, mod, g, b):
    t = xs.shape[0]
    kdim = w.shape[1]
    blocks_per_seg = SEG // OUT_TM
    n_ctx_row = mod.shape[0] - 1
    return pl.pallas_call(
        _out_kernel,
        grid=(t // OUT_TM, kdim // OUT_TK),
        in_specs=[
            pl.BlockSpec((OUT_TM, OUT_TK), lambda i, k: (i, k)),
            pl.BlockSpec((None, OUT_TK, D_MODEL), lambda i, k: (layer, k, 0)),
            pl.BlockSpec((OUT_TM, D_MODEL), lambda i, k: (i, 0)),
            pl.BlockSpec((1, N_MOD, D_MODEL), lambda i, k: (i // blocks_per_seg, 0, 0)),
            pl.BlockSpec((1, N_MOD, D_MODEL), lambda i, k: (n_ctx_row, 0, 0)),
            pl.BlockSpec((1, D_MODEL), lambda i, k: (0, 0)),
            pl.BlockSpec((1, D_MODEL), lambda i, k: (0, 0)),
        ],
        out_specs=pl.BlockSpec((OUT_TM, D_MODEL), lambda i, k: (i, 0)),
        out_shape=jax.ShapeDtypeStruct((t, D_MODEL), F32),
        scratch_shapes=[pltpu.VMEM((OUT_TM, D_MODEL), F32)],
        compiler_params=_params(2),
        name="mixer_out",
    )(y, w, xs, mod, mod, g.reshape(1, D_MODEL), b.reshape(1, D_MODEL))


def _retention_kernel(ld_ref, q_ref, k_ref, v_ref, g_ref, cos_ref, sin_ref, y_ref,
                      qs_ref, ks_ref, o_ref, sf_ref, sb_ref):
    h = pl.program_id(1)
    lg_f = ld_ref[0, h]
    lg_b = ld_ref[1, h]
    half = RET_DK // 2
    c = RET_CHUNK

    def rows_of(chunk):
        return pl.ds(pl.multiple_of(chunk * c, c), c)

    def rotate(n, carry):
        r = rows_of(n)
        cos = cos_ref[r, :]
        sin = sin_ref[r, :]
        for src, dst, scale in ((q_ref, qs_ref, 1.0), (k_ref, ks_ref, RET_DK ** -0.5)):
            t1 = src[r, 0:half] * scale
            t2 = src[r, half:RET_DK] * scale
            dst[r, 0:half] = t1 * cos - t2 * sin
            dst[r, half:RET_DK] = t1 * sin + t2 * cos
        return carry

    lax.fori_loop(0, RET_LAT_CHUNKS, rotate, 0)
    qs_ref[SEQ:SEG, :] = q_ref[SEQ:SEG, :]
    ks_ref[SEQ:SEG, :] = k_ref[SEQ:SEG, :] * RET_DK ** -0.5

    ri = lax.broadcasted_iota(jnp.int32, (c, c), 0)
    ci = lax.broadcasted_iota(jnp.int32, (c, c), 1)
    rel = (ri - ci).astype(F32)
    intra = jnp.where(ri >= ci, jnp.exp(jnp.maximum(rel, 0.0) * lg_f), 0.0) \
        + jnp.where(ri <= ci, jnp.exp(jnp.maximum(-rel, 0.0) * lg_b), 0.0)
    idx = lax.broadcasted_iota(jnp.int32, (c, 1), 0).astype(F32)
    qdec_f = jnp.exp((idx + 1.0) * lg_f)
    kdec_f = jnp.exp((c - 1.0 - idx) * lg_f)
    qdec_b = jnp.exp((c - idx) * lg_b)
    kdec_b = jnp.exp(idx * lg_b)
    cdec_f = jnp.exp(c * lg_f)
    cdec_b = jnp.exp(c * lg_b)

    sf_ref[...] = jnp.zeros_like(sf_ref)
    sb_ref[...] = jnp.zeros_like(sb_ref)

    o_ref[...] = jnp.zeros_like(o_ref)

    def body(n, carry):
        n_ctx = RET_NCHUNK - RET_LAT_CHUNKS
        cf = jnp.where(n < n_ctx, RET_LAT_CHUNKS + n, n - n_ctx)
        cb = RET_NCHUNK - 1 - n
        rf = rows_of(cf)
        rb = rows_of(cb)
        qf = qs_ref[rf, :]
        kf = ks_ref[rf, :]
        vf = v_ref[rf, :]
        qb = qs_ref[rb, :]
        kb = ks_ref[rb, :]
        vb = v_ref[rb, :]
        sf = sf_ref[...]
        sb = sb_ref[...]
        scores = _dot_nt(qf, kf) * intra
        cross_f = _dot(qf * qdec_f, sf)
        cross_b = _dot(qb * qdec_b, sb)
        kv_f = _dot_tn(kf * kdec_f, vf)
        kv_b = _dot_tn(kb * kdec_b, vb)
        o_ref[rf, :] += _dot(scores, vf) + cross_f
        o_ref[rb, :] += cross_b
        sf_ref[...] = sf * cdec_f + kv_f
        sb_ref[...] = sb * cdec_b + kv_b
        return carry

    lax.fori_loop(0, RET_NCHUNK, body, 0)

    def finish(n, carry):
        r = rows_of(n)
        o = o_ref[r, :]
        mu = jnp.mean(o, axis=-1, keepdims=True)
        oc = o - mu
        var = jnp.mean(oc * oc, axis=-1, keepdims=True)
        y_ref[r, :] = (oc * lax.rsqrt(var + LN_EPS) * _silu(g_ref[r, :])).astype(BF16)
        return carry

    lax.fori_loop(0, RET_NCHUNK, finish, 0)


def _retention(p, log_decay, cos, sin, batch):
    kb = RET_QK // RET_DK
    vb = 2 * RET_QK // RET_DV
    gb = vb + RET_HEADS
    half = RET_DK // 2
    return pl.pallas_call(
        _retention_kernel,
        grid=(batch, RET_HEADS),
        in_specs=[
            pl.BlockSpec(memory_space=pltpu.SMEM),
            pl.BlockSpec((SEG, RET_DK), lambda b, h: (b, h)),
            pl.BlockSpec((SEG, RET_DK), lambda b, h: (b, kb + h)),
            pl.BlockSpec((SEG, RET_DV), lambda b, h: (b, vb + h)),
            pl.BlockSpec((SEG, RET_DV), lambda b, h: (b, gb + h)),
            pl.BlockSpec((SEQ, half), lambda b, h: (0, 0)),
            pl.BlockSpec((SEQ, half), lambda b, h: (0, 0)),
        ],
        out_specs=pl.BlockSpec((SEG, RET_DV), lambda b, h: (b, h)),
        out_shape=jax.ShapeDtypeStruct((batch * SEG, RET_VW), BF16),
        scratch_shapes=[pltpu.VMEM((SEG, RET_DK), F32),
                        pltpu.VMEM((SEG, RET_DK), F32),
                        pltpu.VMEM((SEG, RET_DV), F32),
                        pltpu.VMEM((RET_DK, RET_DV), F32),
                        pltpu.VMEM((RET_DK, RET_DV), F32)],
        compiler_params=_params(2),
        name="retention",
    )(log_decay, p, p, p, p, cos, sin)


def _axial_rotary():
    rows = SEQ // GRID_W
    pos_r = jnp.repeat(jnp.arange(rows), GRID_W).astype(F32)
    pos_c = jnp.tile(jnp.arange(GRID_W), rows).astype(F32)
    half = RET_DK // 2
    inv = ROPE_BASE ** (-jnp.arange(0, half, 2, dtype=F32) / half)
    ang = jnp.concatenate([pos_r[:, None] * inv, pos_c[:, None] * inv], -1)
    return jnp.cos(ang), jnp.sin(ang)


def _ret_w_kernel(w_ref, o_ref):
    n = RET_DK
    r = lax.broadcasted_iota(jnp.int32, (n, n), 0)
    col = lax.broadcasted_iota(jnp.int32, (n, n), 1)
    src = jnp.where(col < n // 2, 2 * col, 2 * (col - n // 2) + 1)
    is_qk = pl.program_id(1) < 2 * RET_HEADS
    pmat = (r == jnp.where(is_qk, src, col)).astype(BF16)
    o_ref[0] = _dot(w_ref[0], pmat).astype(BF16)


def _prep_ret_w_in(w):
    layers = w.shape[0]
    return pl.pallas_call(
        _ret_w_kernel,
        grid=(layers, RET_IN // RET_DK),
        in_specs=[pl.BlockSpec((1, D_MODEL, RET_DK), lambda l, j: (l, 0, j))],
        out_specs=pl.BlockSpec((1, D_MODEL, RET_DK), lambda l, j: (l, 0, j)),
        out_shape=jax.ShapeDtypeStruct((layers, D_MODEL, RET_IN), BF16),
        compiler_params=_params(2),
        name="ret_w_in_prep",
    )(w)


def _gates_kernel(ba_ref, alog_ref, dtb_ref, o_ref, t_ref):
    rows = GATE_GROUP
    lane = lax.broadcasted_iota(jnp.int32, (1, LANES), 1)
    is_beta = (lane // DN_V_HEADS) % 2 == 0
    ri = lax.broadcasted_iota(jnp.int32, (rows, rows), 0)
    ci = lax.broadcasted_iota(jnp.int32, (rows, rows), 1)
    same = (ri // DN_CHUNK) == (ci // DN_CHUNK)
    tri_prefix = jnp.where(same & (ri >= ci), 1.0, 0.0).astype(BF16)
    tri_suffix = jnp.where(same & (ri <= ci), 1.0, 0.0).astype(BF16)
    neg_a = -jnp.exp(alog_ref[...])
    dtb = dtb_ref[...]
    for grp in range(SEG // rows):
        sl = slice(grp * rows, (grp + 1) * rows)
        ba = ba_ref[sl, :]
        beta = jax.nn.sigmoid(ba)
        t = ba + dtb
        g = neg_a * (jnp.maximum(t, 0.0) + jnp.log1p(jnp.exp(-jnp.abs(t))))
        hi = g.astype(BF16)
        r1 = g - hi.astype(F32)
        mid = r1.astype(BF16)
        lo = (r1 - mid.astype(F32)).astype(BF16)
        prefix = sum(jnp.dot(tri_prefix, p, preferred_element_type=F32) for p in (hi, mid, lo))
        suffix = sum(jnp.dot(tri_suffix, p, preferred_element_type=F32) for p in (hi, mid, lo))
        gc = jnp.where(lane < 2 * DN_V_HEADS, prefix, suffix)
        o_ref[sl, :] = jnp.where(is_beta, beta, gc)
        t_ref[sl, :] = prefix + suffix - g


def _dn_gates(ba, a_log, dt_bias, batch):
    zeros = jnp.zeros((DN_V_HEADS,), F32)
    alog = jnp.concatenate([zeros, a_log[0], zeros, a_log[1]]).reshape(1, LANES)
    dtb = jnp.concatenate([zeros, dt_bias[0], zeros, dt_bias[1]]).reshape(1, LANES)
    shape = jax.ShapeDtypeStruct((batch * SEG, LANES), F32)
    return pl.pallas_call(
        _gates_kernel,
        grid=(batch,),
        in_specs=[
            pl.BlockSpec((SEG, LANES), lambda b: (b, 0)),
            pl.BlockSpec((1, LANES), lambda b: (0, 0)),
            pl.BlockSpec((1, LANES), lambda b: (0, 0)),
        ],
        out_specs=[pl.BlockSpec((SEG, LANES), lambda b: (b, 0)),
                   pl.BlockSpec((SEG, LANES), lambda b: (b, 0))],
        out_shape=[shape, shape],
        compiler_params=_params(1),
        name="dn_gates",
    )(ba, alog, dtb)


def _gate_layouts(gates, totals, batch):
    c = DN_CHUNK
    g = gates.reshape(batch, SEG, 2, 2, DN_K_HEADS, 2)
    gcol = g.transpose(0, 4, 1, 2, 3, 5).reshape(batch, DN_K_HEADS, SEG, 8)
    gr = g.reshape(batch, DN_NCHUNK, c, 2, 2, DN_K_HEADS, 2)
    pair = gr.transpose(0, 5, 1, 3, 4, 6, 2).reshape(batch, DN_K_HEADS, DN_NCHUNK, 2, 2, 2 * c)
    tt = totals.reshape(batch, DN_NCHUNK, c, 2, 2, DN_K_HEADS, 2)[:, :, :, :, 1]
    tot_pair = tt.transpose(0, 4, 1, 3, 5, 2).reshape(batch, DN_K_HEADS, DN_NCHUNK, 2, 1, 2 * c)
    tot_head = jnp.broadcast_to(
        tt[:, :, 0].transpose(0, 3, 1, 2, 4)[..., None], (batch, DN_K_HEADS, DN_NCHUNK, 2, 2, 2 * c))
    pad = jnp.zeros((batch, DN_K_HEADS, DN_NCHUNK, 2, 3, 2 * c), F32)
    grow = jnp.concatenate([pair, tot_pair, tot_head, pad], axis=4)
    return gcol, grow


DN_PAIRS = 2 * DN_NCHUNK
DN_LEVELS = DN_CHUNK.bit_length() - 1
DN_PREP_UNROLL = 6
DN_INV_UNROLL = DN_PAIRS
DN_SOLVE_UNROLL = 12


def _short_conv_silu(x, w):
    n = x.shape[0]
    row = lax.broadcasted_iota(jnp.int32, (n, 1), 0)
    seg_lo = jnp.where(row >= SEQ, SEQ, 0)
    seg_hi = jnp.where(row >= SEQ, SEG, SEQ)
    acc = x * w[DN_CONV // 2:DN_CONV // 2 + 1, :]
    for tap in range(DN_CONV):
        d = tap - DN_CONV // 2
        if d == 0:
            continue
        shifted = pltpu.roll(x, (-d) % n, 0)
        ok = (row + d >= seg_lo) & (row + d < seg_hi)
        acc = acc + jnp.where(ok, shifted, 0.0) * w[tap:tap + 1, :]
    return _silu(acc)


def _l2norm(t):
    return t * lax.rsqrt(jnp.sum(t * t, axis=-1, keepdims=True) + NORM_EPS)


def _block_diag(a, left, right):
    return jnp.concatenate([a * left, a * right], axis=0)


def _deltanet_kernel(q_ref, k_ref, v_ref, z_ref, cwq_ref, cwk_ref, cwv_ref, gcol_ref, grow_ref,
                     nw_ref, y_ref,
                     qn_ref, kn_ref, vs_ref, o_ref, s_ref, l_ref, x_ref, lo_ref, kt_ref, u_ref, w_ref):
    c = DN_CHUNK
    nc = DN_NCHUNK
    dv = DN_DV

    qn_ref[...] = (_l2norm(_short_conv_silu(q_ref[...], cwq_ref[...])) * DN_DK ** -0.5).astype(BF16)
    kn_ref[...] = _l2norm(_short_conv_silu(k_ref[...], cwk_ref[...])).astype(BF16)
    for j in range(2):
        cols = slice(j * dv, (j + 1) * dv)
        vs_ref[:, cols] = _short_conv_silu(v_ref[:, cols], cwv_ref[:, cols]).astype(BF16)
    o_ref[...] = jnp.zeros_like(o_ref)
    s_ref[...] = jnp.zeros_like(s_ref)

    ri = lax.broadcasted_iota(jnp.int32, (c, 2 * c), 0)
    li = lax.broadcasted_iota(jnp.int32, (c, 2 * c), 1)
    cj = li & (c - 1)
    is_left = li < c
    left = is_left.astype(BF16)
    right = 1 - left
    eye2 = (ri == cj).astype(F32)
    incl = (ri >= cj, ri <= cj)
    strict = (ri > cj, ri < cj)

    def level_mask(lv):
        return ((ri >> (lv + 1)) == (cj >> (lv + 1))) & ((ri >> lv) != (cj >> lv))

    eye_k = (lax.broadcasted_iota(jnp.int32, (2 * c, 2 * c), 0)
             == lax.broadcasted_iota(jnp.int32, (2 * c, 2 * c), 1)).astype(BF16)
    zeros_half = jnp.zeros((c, dv), BF16)

    def rows_of(chunk):
        return pl.ds(pl.multiple_of(chunk * c, c), c)

    def diag2(a_l, a_r):
        return jnp.concatenate([jnp.concatenate([a_l, zeros_half], axis=1),
                                jnp.concatenate([zeros_half, a_r], axis=1)], axis=0)

    def prep(chunk):
        rows = rows_of(chunk)
        kc = kn_ref[rows, :]
        qc = qn_ref[rows, :]
        kk2 = jnp.concatenate([kc, kc], axis=0)
        sq = _dot_nt(jnp.concatenate([kc, qc], axis=0), kk2)
        kt2 = _dot_nt(eye_k, kk2)
        gcol = gcol_ref[rows, :]
        for d in range(2):
            p = d * nc + chunk
            gr = grow_ref[chunk, d]
            beta_col = jnp.where(is_left, gcol[:, 4 * d:4 * d + 1], gcol[:, 4 * d + 1:4 * d + 2])
            gc_col = jnp.where(is_left, gcol[:, 4 * d + 2:4 * d + 3], gcol[:, 4 * d + 3:4 * d + 4])
            gc_row = gr[1:2, :]
            tot_row = gr[2:3, :]
            decay = jnp.where(incl[d], jnp.exp(jnp.where(incl[d], gc_col - gc_row, 0.0)), 0.0)
            lmat = jnp.where(strict[d], sq[0:c, :] * beta_col * decay, 0.0)
            l_ref[p] = lmat.astype(BF16)
            x_ref[p] = eye2 - jnp.where(level_mask(0), lmat, 0.0)
            lo_ref[p] = jnp.concatenate([eye2 * jnp.exp(gc_row), sq[c:2 * c, :] * decay], axis=1).astype(BF16)
            kt_ref[p] = (kt2 * jnp.exp(tot_row - gc_row)).astype(BF16)

    def prep_body(i, carry):
        for u in range(DN_PREP_UNROLL):
            prep(i * DN_PREP_UNROLL + u)
        return carry

    lax.fori_loop(0, nc // DN_PREP_UNROLL, prep_body, 0)

    for lv in range(1, DN_LEVELS):
        lmask = level_mask(lv).astype(BF16)

        def inv_body(i, carry, lmask=lmask):
            stage = []
            for u in range(DN_INV_UNROLL):
                p = i * DN_INV_UNROLL + u
                x = x_ref[p]
                xb = x.astype(BF16)
                stage.append((p, x, xb, _dot(l_ref[p] * lmask, _block_diag(xb, left, right))))
            for p, x, xb, y in stage:
                x_ref[p] = x - _dot(xb, _block_diag(y.astype(BF16), left, right))
            return carry

        lax.fori_loop(0, DN_PAIRS // DN_INV_UNROLL, inv_body, 0)

    def solve_body(i, carry):
        for u in range(DN_SOLVE_UNROLL):
            p = i * DN_SOLVE_UNROLL + u
            d = p // nc
            chunk = p - d * nc
            rows = rows_of(chunk)
            gr = grow_ref[chunk, d]
            beta_row = gr[0:1, :]
            x = x_ref[p]
            tu = x * beta_row
            tw = x * (beta_row * jnp.exp(gr[1:2, :]))
            kc = kn_ref[rows, :]
            u_ref[p] = _dot(tu, diag2(vs_ref[rows, 0:dv], vs_ref[rows, dv:2 * dv]))
            w_ref[p] = _dot(tw, diag2(kc, kc)).astype(BF16)
        return carry

    lax.fori_loop(0, DN_PAIRS // DN_SOLVE_UNROLL, solve_body, 0)

    def scan_body(n, carry):
        n_ctx = nc - DN_LAT_CHUNKS
        chunk_f = jnp.where(n < n_ctx, DN_LAT_CHUNKS + n, n - n_ctx)
        chunk_b = nc - 1 - n
        stage = []
        for d, chunk in ((0, chunk_f), (1, chunk_b)):
            p = d * nc + chunk
            qc = qn_ref[rows_of(chunk), :]
            w2 = w_ref[p]
            s_old = [s_ref[2 * d + j] for j in range(2)]
            wq = [_dot(jnp.concatenate([w2[:, j * dv:(j + 1) * dv], qc], axis=0), s_old[j])
                  for j in range(2)]
            stage.append((d, chunk, p, s_old, wq))
        for d, chunk, p, s_old, wq in stage:
            rows = rows_of(chunk)
            u2 = u_ref[p]
            gr = grow_ref[chunk, d]
            vn = [(u2[:, j * dv:(j + 1) * dv] - wq[j][0:c, :]).astype(BF16) for j in range(2)]
            qs = [wq[j][c:2 * c, :].astype(BF16) for j in range(2)]
            rhs = jnp.concatenate([diag2(qs[0], qs[1]), diag2(vn[0], vn[1])], axis=0)
            o_ref[d, rows, :] += _dot(lo_ref[p], rhs)
            ds = _dot(kt_ref[p], diag2(vn[0], vn[1]))
            for j in range(2):
                s_ref[2 * d + j] = s_old[j] * jnp.exp(gr[3 + j:4 + j, :]) + ds[:, j * dv:(j + 1) * dv]
        return carry

    lax.fori_loop(0, nc, scan_body, 0)

    nw = nw_ref[...]
    for j in range(2):
        cols = slice(j * dv, (j + 1) * dv)
        o = o_ref[0, :, cols] + o_ref[1, :, cols]
        rms = o * lax.rsqrt(jnp.mean(o * o, axis=-1, keepdims=True) + NORM_EPS)
        y_ref[:, cols] = (rms * nw * _silu(z_ref[:, cols])).astype(BF16)


def _deltanet(p, conv_w, gcol, grow, norm_w, batch):
    kb = DN_QK // DN_DK
    vb = 2 * DN_QK // (2 * DN_DV)
    zb = DN_CONV_CH // (2 * DN_DV)
    pair = 2 * DN_DV
    c = DN_CHUNK
    return pl.pallas_call(
        _deltanet_kernel,
        grid=(batch, DN_K_HEADS),
        in_specs=[
            pl.BlockSpec((SEG, DN_DK), lambda b, h: (b, h)),
            pl.BlockSpec((SEG, DN_DK), lambda b, h: (b, kb + h)),
            pl.BlockSpec((SEG, pair), lambda b, h: (b, vb + h)),
            pl.BlockSpec((SEG, pair), lambda b, h: (b, zb + h)),
            pl.BlockSpec((DN_CONV, DN_DK), lambda b, h: (0, h)),
            pl.BlockSpec((DN_CONV, DN_DK), lambda b, h: (0, kb + h)),
            pl.BlockSpec((DN_CONV, pair), lambda b, h: (0, vb + h)),
            pl.BlockSpec((None, None, SEG, 8), lambda b, h: (b, h, 0, 0)),
            pl.BlockSpec((None, None, DN_NCHUNK, 2, 8, 2 * c), lambda b, h: (b, h, 0, 0, 0, 0)),
            pl.BlockSpec((1, DN_DV), lambda b, h: (0, 0)),
        ],
        out_specs=pl.BlockSpec((SEG, pair), lambda b, h: (b, h)),
        out_shape=jax.ShapeDtypeStruct((batch * SEG, DN_VW), BF16),
        scratch_shapes=[pltpu.VMEM((SEG, DN_DK), BF16),
                        pltpu.VMEM((SEG, DN_DK), BF16),
                        pltpu.VMEM((SEG, pair), BF16),
                        pltpu.VMEM((2, SEG, pair), F32),
                        pltpu.VMEM((4, DN_DK, DN_DV), F32),
                        pltpu.VMEM((DN_PAIRS, c, 2 * c), BF16),
                        pltpu.VMEM((DN_PAIRS, c, 2 * c), F32),
                        pltpu.VMEM((DN_PAIRS, c, 4 * c), BF16),
                        pltpu.VMEM((DN_PAIRS, 2 * c, 2 * c), BF16),
                        pltpu.VMEM((DN_PAIRS, c, pair), F32),
                        pltpu.VMEM((DN_PAIRS, c, pair), BF16)],
        compiler_params=_params(2),
        name="deltanet",
    )(p, p, p, p, conv_w, conv_w, conv_w, gcol, grow, norm_w.reshape(1, DN_DV))


def kernel(x, c, ctx, c_ctx, mod_w, mod_b, ln_g, ln_b, ffn_w_in, ffn_w_out, ret_w_in, ret_log_decay,
           ret_w_out, dn_w_in, dn_conv_w, dn_a_log, dn_dt_bias, dn_norm_w, dn_w_out):
    batch = x.shape[0]
    xs = jnp.concatenate([x, ctx], axis=1).reshape(batch * SEG, D_MODEL)
    cond = jnp.concatenate([c, c_ctx[None, :]], axis=0)
    mod = _modulation(cond, mod_w, mod_b).reshape(DEPTH, batch + 1, N_MOD, D_MODEL)
    cos, sin = _axial_rotary()
    ffn_in = ffn_w_in.astype(BF16)
    ffn_out = ffn_w_out.astype(BF16)
    ret_in = _prep_ret_w_in(ret_w_in)
    ret_out = ret_w_out.astype(BF16)
    dn_in = dn_w_in.astype(BF16)
    dn_out = dn_w_out.astype(BF16)
    dn_main_blocks = (DN_CONV_CH + DN_VW) // PROJ_TN

    for i in range(DEPTH):
        j = i // N_MIXERS
        m = mod[i]
        xs = _ffn_step(xs, m, i, 0, ffn_in, ffn_out, ln_g[i, 0], ln_b[i, 0])
        if i % N_MIXERS == 0:
            p = _mixer_proj(xs, m, ret_in, j, PROJ_TN)
            y = _retention(p, ret_log_decay[j], cos, sin, batch)
            xs = _mixer_out(y, ret_out, j, xs, m, ln_g[i, 1], ln_b[i, 1])
        else:
            p = _mixer_proj(xs, m, dn_in, j, PROJ_TN, 0, dn_main_blocks)
            ba = _mixer_proj(xs, m, dn_in, j, LANES, (DN_CONV_CH + DN_VW) // LANES, 1)
            gates, totals = _dn_gates(ba, dn_a_log[j], dn_dt_bias[j], batch)
            gcol, grow = _gate_layouts(gates, totals, batch)
            y = _deltanet(p, dn_conv_w[j], gcol, grow, dn_norm_w[j], batch)
            xs = _mixer_out(y, dn_out, j, xs, m, ln_g[i, 1], ln_b[i, 1])
        xs = _ffn_step(xs, m, i, 1, ffn_in, ffn_out, ln_g[i, 2], ln_b[i, 2])

    return xs.reshape(batch, SEG, D_MODEL)[:, :SEQ]
```

```python
import functools

import jax
import jax.numpy as jnp
from jax import lax
from jax.experimental import pallas as pl
from jax.experimental.pallas import tpu as pltpu

D_MODEL = 2048
SEQ = 2048
DEPTH = 4
GRID_W = 64
CTX_LEN = 256
SEG = SEQ + CTX_LEN
N_MIXERS = 2
RET_HEADS = D_MODEL // 256
RET_DK = 256
RET_DV = 2 * RET_DK
RET_CHUNK = 128
RET_QK = RET_HEADS * RET_DK
RET_VW = RET_HEADS * RET_DV
RET_IN = 2 * RET_QK + 2 * RET_VW
ROPE_BASE = 10000.0
DN_K_HEADS = D_MODEL // 128
DN_V_HEADS = 2 * DN_K_HEADS
DN_DK = 128
DN_DV = 128
DN_CHUNK = 64
DN_CONV = 5
DN_QK = DN_K_HEADS * DN_DK
DN_VW = DN_V_HEADS * DN_DV
DN_CONV_CH = 2 * DN_QK + DN_VW
DN_IN = DN_CONV_CH + DN_VW + 4 * DN_V_HEADS
FFN_HIDDEN = 5504
N_MOD = 9
ALPHA = (2 * DEPTH) ** 0.25
LN_EPS = 1e-5
NORM_EPS = 1e-6

F32 = jnp.float32
BF16 = jnp.bfloat16

VMEM_LIMIT_BYTES = 56 * 1024 * 1024
LANES = 128

FFN_TH = 512
FFN_TM = 768
PROJ_TM = 1152
PROJ_TN = 1536
OUT_TM = 576
OUT_TK = 4096
MOD_TN = 1024

RET_NCHUNK = SEG // RET_CHUNK
RET_LAT_CHUNKS = SEQ // RET_CHUNK
DN_NCHUNK = SEG // DN_CHUNK
DN_LAT_CHUNKS = SEQ // DN_CHUNK
GATE_GROUP = 256


def _params(n_axes):
    return pltpu.CompilerParams(
        dimension_semantics=("arbitrary",) * n_axes,
        vmem_limit_bytes=VMEM_LIMIT_BYTES)


def _dot(a, b):
    return jnp.dot(a.astype(BF16), b.astype(BF16), preferred_element_type=F32)


def _dot_nt(a, b):
    return lax.dot_general(a.astype(BF16), b.astype(BF16), (((1,), (1,)), ((), ())),
                           preferred_element_type=F32)


def _dot_tn(a, b):
    return lax.dot_general(a.astype(BF16), b.astype(BF16), (((0,), (0,)), ((), ())),
                           preferred_element_type=F32)


def _silu(t):
    return t * jax.nn.sigmoid(t)


def _rows_are_ctx(block, tm):
    blocks_per_seg = SEG // tm
    r0 = (block % blocks_per_seg) * tm
    return r0 + lax.broadcasted_iota(jnp.int32, (tm, 1), 0) >= SEQ


def _pick_mod(is_ctx, mx_ref, mc_ref, r):
    return jnp.where(is_ctx, mc_ref[0, r:r + 1, :], mx_ref[0, r:r + 1, :])


def _layer_norm(z, g, b):
    mu = jnp.mean(z, axis=-1, keepdims=True)
    zc = z - mu
    var = jnp.mean(zc * zc, axis=-1, keepdims=True)
    return zc * lax.rsqrt(var + LN_EPS) * g + b


def _mod_kernel(c_ref, w_ref, b_ref, o_ref):
    a = _silu(c_ref[...])
    o_ref[0] = _dot(a, w_ref[0]) + b_ref[0]


def _modulation(cond, mod_w, mod_b):
    n = N_MOD * D_MODEL
    rows = cond.shape[0]
    return pl.pallas_call(
        _mod_kernel,
        grid=(DEPTH, n // MOD_TN),
        in_specs=[
            pl.BlockSpec((rows, D_MODEL), lambda l, j: (0, 0)),
            pl.BlockSpec((1, D_MODEL, MOD_TN), lambda l, j: (l, 0, j)),
            pl.BlockSpec((1, 1, MOD_TN), lambda l, j: (l, 0, j)),
        ],
        out_specs=pl.BlockSpec((1, rows, MOD_TN), lambda l, j: (l, 0, j)),
        out_shape=jax.ShapeDtypeStruct((DEPTH, rows, n), F32),
        compiler_params=_params(2),
        name="ada_modulation",
    )(cond, mod_w, mod_b.reshape(DEPTH, 1, n))


FFN_STEPS = -(-FFN_HIDDEN // FFN_TH)
FFN_TAIL = FFN_HIDDEN - (FFN_STEPS - 1) * FFN_TH


def _ffn_kernel(x_ref, mx_ref, mc_ref, wg_ref, wu_ref, wo_ref, g_ref, b_ref, o_ref, h_ref, acc_ref, *, sub):
    i = pl.program_id(0)
    k = pl.program_id(1)
    last = pl.num_programs(1) - 1
    is_ctx = _rows_are_ctx(i, FFN_TM)

    @pl.when(k == 0)
    def _():
        shift = _pick_mod(is_ctx, mx_ref, mc_ref, 3 * sub)
        scale = _pick_mod(is_ctx, mx_ref, mc_ref, 3 * sub + 1)
        h_ref[...] = (x_ref[...] * (1 + scale) + shift).astype(BF16)
        acc_ref[...] = jnp.zeros_like(acc_ref)

    def accumulate(wg, wu, wo):
        h = h_ref[...]
        gate = _dot(h, wg)
        up = _dot(h, wu)
        acc_ref[...] += _dot(_silu(gate) * up, wo)

    @pl.when(k < last)
    def _():
        accumulate(wg_ref[...], wu_ref[0, 0], wo_ref[0, 0])

    @pl.when(k == last)
    def _():
        accumulate(wg_ref[:, 0:FFN_TAIL], wu_ref[0, 0, :, FFN_TH - FFN_TAIL:FFN_TH],
                   wo_ref[0, 0, FFN_TH - FFN_TAIL:FFN_TH, :])
        res_gate = _pick_mod(is_ctx, mx_ref, mc_ref, 3 * sub + 2)
        z = ALPHA * x_ref[...] + 0.5 * res_gate * acc_ref[...]
        o_ref[...] = _layer_norm(z, g_ref[...], b_ref[...])


def _ffn_step(xs, mod, layer, half, w_in, w_out, g, b):
    t = xs.shape[0]
    one = pl.Element(1)
    blocks_per_seg = SEG // FFN_TM
    n_ctx_row = mod.shape[0] - 1
    return pl.pallas_call(
        functools.partial(_ffn_kernel, sub=2 * half),
        grid=(t // FFN_TM, FFN_STEPS),
        in_specs=[
            pl.BlockSpec((FFN_TM, D_MODEL), lambda i, k: (i, 0)),
            pl.BlockSpec((1, N_MOD, D_MODEL), lambda i, k: (i // blocks_per_seg, 0, 0)),
            pl.BlockSpec((1, N_MOD, D_MODEL), lambda i, k: (n_ctx_row, 0, 0)),
            pl.BlockSpec((None, None, D_MODEL, FFN_TH), lambda i, k: (layer, half, 0, k)),
            pl.BlockSpec((one, one, pl.Element(D_MODEL), pl.Element(FFN_TH)),
                         lambda i, k: (layer, half, 0, jnp.minimum((FFN_HIDDEN + k * FFN_TH) // LANES,
                                                      (2 * FFN_HIDDEN - FFN_TH) // LANES) * LANES)),
            pl.BlockSpec((one, one, pl.Element(FFN_TH), pl.Element(D_MODEL)),
                         lambda i, k: (layer, half, jnp.minimum(k * FFN_TH // LANES,
                                                                (FFN_HIDDEN - FFN_TH) // LANES) * LANES, 0)),
            pl.BlockSpec((1, D_MODEL), lambda i, k: (0, 0)),
            pl.BlockSpec((1, D_MODEL), lambda i, k: (0, 0)),
        ],
        out_specs=pl.BlockSpec((FFN_TM, D_MODEL), lambda i, k: (i, 0)),
        out_shape=jax.ShapeDtypeStruct((t, D_MODEL), F32),
        scratch_shapes=[pltpu.VMEM((FFN_TM, D_MODEL), BF16),
                        pltpu.VMEM((FFN_TM, D_MODEL), F32)],
        compiler_params=_params(2),
        name="ffn_step",
    )(xs, mod, mod, w_in, w_in, w_out, g.reshape(1, D_MODEL), b.reshape(1, D_MODEL))


def _proj_kernel(x_ref, mx_ref, mc_ref, w_ref, o_ref, h_ref):
    i = pl.program_id(0)

    @pl.when(pl.program_id(1) == 0)
    def _():
        is_ctx = _rows_are_ctx(i, PROJ_TM)
        shift = _pick_mod(is_ctx, mx_ref, mc_ref, 3)
        scale = _pick_mod(is_ctx, mx_ref, mc_ref, 4)
        h_ref[...] = (x_ref[...] * (1 + scale) + shift).astype(BF16)

    o_ref[...] = jnp.dot(h_ref[...], w_ref[...], preferred_element_type=F32)


def _mixer_proj(xs, mod, w, layer, tn, first_block=0, n_blocks=None):
    t = xs.shape[0]
    if n_blocks is None:
        n_blocks = w.shape[2] // tn
    blocks_per_seg = SEG // PROJ_TM
    n_ctx_row = mod.shape[0] - 1
    return pl.pallas_call(
        _proj_kernel,
        grid=(t // PROJ_TM, n_blocks),
        in_specs=[
            pl.BlockSpec((PROJ_TM, D_MODEL), lambda i, j: (i, 0)),
            pl.BlockSpec((1, N_MOD, D_MODEL), lambda i, j: (i // blocks_per_seg, 0, 0)),
            pl.BlockSpec((1, N_MOD, D_MODEL), lambda i, j: (n_ctx_row, 0, 0)),
            pl.BlockSpec((None, D_MODEL, tn), lambda i, j: (layer, 0, first_block + j)),
        ],
        out_specs=pl.BlockSpec((PROJ_TM, tn), lambda i, j: (i, j)),
        out_shape=jax.ShapeDtypeStruct((t, n_blocks * tn), F32),
        scratch_shapes=[pltpu.VMEM((PROJ_TM, D_MODEL), BF16)],
        compiler_params=_params(2),
        name="mixer_proj",
    )(xs, mod, mod, w)


def _out_kernel(y_ref, w_ref, x_ref, mx_ref, mc_ref, g_ref, b_ref, o_ref, acc_ref):
    i = pl.program_id(0)
    k = pl.program_id(1)

    @pl.when(k == 0)
    def _():
        acc_ref[...] = jnp.zeros_like(acc_ref)

    acc_ref[...] += jnp.dot(y_ref[...], w_ref[...], preferred_element_type=F32)

    @pl.when(k == pl.num_programs(1) - 1)
    def _():
        is_ctx = _rows_are_ctx(i, OUT_TM)
        gate = _pick_mod(is_ctx, mx_ref, mc_ref, 5)
        z = ALPHA * x_ref[...] + gate * acc_ref[...]
        o_ref[...] = _layer_norm(z, g_ref[...], b_ref[...])


def _mixer_out(y, w, layer, xs, mod, g, b):
    t = xs.shape[0]
    kdim = w.shape[1]
    blocks_per_seg = SEG // OUT_TM
    n_ctx_row = mod.shape[0] - 1
    return pl.pallas_call(
        _out_kernel,
        grid=(t // OUT_TM, kdim // OUT_TK),
        in_specs=[
            pl.BlockSpec((OUT_TM, OUT_TK), lambda i, k: (i, k)),
            pl.BlockSpec((None, OUT_TK, D_MODEL), lambda i, k: (layer, k, 0),
                         pipeline_mode=pl.Buffered(1)),
            pl.BlockSpec((OUT_TM, D_MODEL), lambda i, k: (i, 0)),
            pl.BlockSpec((1, N_MOD, D_MODEL), lambda i, k: (i // blocks_per_seg, 0, 0)),
            pl.BlockSpec((1, N_MOD, D_MODEL), lambda i, k: (n_ctx_row, 0, 0)),
            pl.BlockSpec((1, D_MODEL), lambda i, k: (0, 0)),
            pl.BlockSpec((1, D_MODEL), lambda i, k: (0, 0)),
        ],
        out_specs=pl.BlockSpec((OUT_TM, D_MODEL), lambda i, k: (i, 0)),
        out_shape=jax.ShapeDtypeStruct((t, D_MODEL), F32),
        scratch_shapes=[pltpu.VMEM((OUT_TM, D_MODEL), F32)],
        compiler_params=_params(2),
        name="mixer_out",
    )(y, w, xs, mod, mod, g.reshape(1, D_MODEL), b.reshape(1, D_MODEL))


def _retention_kernel(ld_ref, q_ref, k_ref, v_ref, g_ref, cos_ref, sin_ref, y_ref,
                      qs_ref, ks_ref, o_ref, sf_ref, sb_ref):
    h = pl.program_id(1)
    lg_f = ld_ref[0, h]
    lg_b = ld_ref[1, h]
    half = RET_DK // 2
    c = RET_CHUNK

    def rows_of(chunk):
        return pl.ds(pl.multiple_of(chunk * c, c), c)

    def rotate(n, carry):
        r = rows_of(n)
        cos = cos_ref[r, :]
        sin = sin_ref[r, :]
        for src, dst, scale in ((q_ref, qs_ref, 1.0), (k_ref, ks_ref, RET_DK ** -0.5)):
            t1 = src[r, 0:half] * scale
            t2 = src[r, half:RET_DK] * scale
            dst[r, 0:half] = t1 * cos - t2 * sin
            dst[r, half:RET_DK] = t1 * sin + t2 * cos
        return carry

    lax.fori_loop(0, RET_LAT_CHUNKS, rotate, 0)
    qs_ref[SEQ:SEG, :] = q_ref[SEQ:SEG, :]
    ks_ref[SEQ:SEG, :] = k_ref[SEQ:SEG, :] * RET_DK ** -0.5

    ri = lax.broadcasted_iota(jnp.int32, (c, c), 0)
    ci = lax.broadcasted_iota(jnp.int32, (c, c), 1)
    rel = (ri - ci).astype(F32)
    intra = jnp.where(ri >= ci, jnp.exp(jnp.maximum(rel, 0.0) * lg_f), 0.0) \
        + jnp.where(ri <= ci, jnp.exp(jnp.maximum(-rel, 0.0) * lg_b), 0.0)
    idx = lax.broadcasted_iota(jnp.int32, (c, 1), 0).astype(F32)
    qdec_f = jnp.exp((idx + 1.0) * lg_f)
    kdec_f = jnp.exp((c - 1.0 - idx) * lg_f)
    qdec_b = jnp.exp((c - idx) * lg_b)
    kdec_b = jnp.exp(idx * lg_b)
    cdec_f = jnp.exp(c * lg_f)
    cdec_b = jnp.exp(c * lg_b)

    sf_ref[...] = jnp.zeros_like(sf_ref)
    sb_ref[...] = jnp.zeros_like(sb_ref)

    o_ref[...] = jnp.zeros_like(o_ref)

    def body(n, carry):
        n_ctx = RET_NCHUNK - RET_LAT_CHUNKS
        cf = jnp.where(n < n_ctx, RET_LAT_CHUNKS + n, n - n_ctx)
        cb = RET_NCHUNK - 1 - n
        rf = rows_of(cf)
        rb = rows_of(cb)
        qf = qs_ref[rf, :]
        kf = ks_ref[rf, :]
        vf = v_ref[rf, :]
        qb = qs_ref[rb, :]
        kb = ks_ref[rb, :]
        vb = v_ref[rb, :]
        sf = sf_ref[...]
        sb = sb_ref[...]
        scores = _dot_nt(qf, kf) * intra
        cross_f = _dot(qf * qdec_f, sf)
        cross_b = _dot(qb * qdec_b, sb)
        kv_f = _dot_tn(kf * kdec_f, vf)
        kv_b = _dot_tn(kb * kdec_b, vb)
        o_ref[rf, :] += _dot(scores, vf) + cross_f
        o_ref[rb, :] += cross_b
        sf_ref[...] = sf * cdec_f + kv_f
        sb_ref[...] = sb * cdec_b + kv_b
        return carry

    lax.fori_loop(0, RET_NCHUNK, body, 0)

    def finish(n, carry):
        r = rows_of(n)
        o = o_ref[r, :]
        mu = jnp.mean(o, axis=-1, keepdims=True)
        oc = o - mu
        var = jnp.mean(oc * oc, axis=-1, keepdims=True)
        y_ref[r, :] = (oc * lax.rsqrt(var + LN_EPS) * _silu(g_ref[r, :])).astype(BF16)
        return carry

    lax.fori_loop(0, RET_NCHUNK, finish, 0)


def _retention(p, log_decay, cos, sin, batch):
    kb = RET_QK // RET_DK
    vb = 2 * RET_QK // RET_DV
    gb = vb + RET_HEADS
    half = RET_DK // 2
    return pl.pallas_call(
        _retention_kernel,
        grid=(batch, RET_HEADS),
        in_specs=[
            pl.BlockSpec(memory_space=pltpu.SMEM),
            pl.BlockSpec((SEG, RET_DK), lambda b, h: (b, h)),
            pl.BlockSpec((SEG, RET_DK), lambda b, h: (b, kb + h)),
            pl.BlockSpec((SEG, RET_DV), lambda b, h: (b, vb + h)),
            pl.BlockSpec((SEG, RET_DV), lambda b, h: (b, gb + h)),
            pl.BlockSpec((SEQ, half), lambda b, h: (0, 0)),
            pl.BlockSpec((SEQ, half), lambda b, h: (0, 0)),
        ],
        out_specs=pl.BlockSpec((SEG, RET_DV), lambda b, h: (b, h)),
        out_shape=jax.ShapeDtypeStruct((batch * SEG, RET_VW), BF16),
        scratch_shapes=[pltpu.VMEM((SEG, RET_DK), F32),
                        pltpu.VMEM((SEG, RET_DK), F32),
                        pltpu.VMEM((SEG, RET_DV), F32),
                        pltpu.VMEM((RET_DK, RET_DV), F32),
                        pltpu.VMEM((RET_DK, RET_DV), F32)],
        compiler_params=_params(2),
        name="retention",
    )(log_decay, p, p, p, p, cos, sin)


def _axial_rotary():
    rows = SEQ // GRID_W
    pos_r = jnp.repeat(jnp.arange(rows), GRID_W).astype(F32)
    pos_c = jnp.tile(jnp.arange(GRID_W), rows).astype(F32)
    half = RET_DK // 2
    inv = ROPE_BASE ** (-jnp.arange(0, half, 2, dtype=F32) / half)
    ang = jnp.concatenate([pos_r[:, None] * inv, pos_c[:, None] * inv], -1)
    return jnp.cos(ang), jnp.sin(ang)


def _ret_w_kernel(w_ref, o_ref):
    n = RET_DK
    r = lax.broadcasted_iota(jnp.int32, (n, n), 0)
    col = lax.broadcasted_iota(jnp.int32, (n, n), 1)
    src = jnp.where(col < n // 2, 2 * col, 2 * (col - n // 2) + 1)
    is_qk = pl.program_id(1) < 2 * RET_HEADS
    pmat = (r == jnp.where(is_qk, src, col)).astype(BF16)
    o_ref[0] = _dot(w_ref[0], pmat).astype(BF16)


def _prep_ret_w_in(w):
    layers = w.shape[0]
    return pl.pallas_call(
        _ret_w_kernel,
        grid=(layers, RET_IN // RET_DK),
        in_specs=[pl.BlockSpec((1, D_MODEL, RET_DK), lambda l, j: (l, 0, j))],
        out_specs=pl.BlockSpec((1, D_MODEL, RET_DK), lambda l, j: (l, 0, j)),
        out_shape=jax.ShapeDtypeStruct((layers, D_MODEL, RET_IN), BF16),
        compiler_params=_params(2),
        name="ret_w_in_prep",
    )(w)


def _gates_kernel(ba_ref, alog_ref, dtb_ref, o_ref, t_ref):
    rows = GATE_GROUP
    lane = lax.broadcasted_iota(jnp.int32, (1, LANES), 1)
    is_beta = (lane // DN_V_HEADS) % 2 == 0
    ri = lax.broadcasted_iota(jnp.int32, (rows, rows), 0)
    ci = lax.broadcasted_iota(jnp.int32, (rows, rows), 1)
    same = (ri // DN_CHUNK) == (ci // DN_CHUNK)
    tri_prefix = jnp.where(same & (ri >= ci), 1.0, 0.0).astype(BF16)
    tri_suffix = jnp.where(same & (ri <= ci), 1.0, 0.0).astype(BF16)
    neg_a = -jnp.exp(alog_ref[...])
    dtb = dtb_ref[...]
    for grp in range(SEG // rows):
        sl = slice(grp * rows, (grp + 1) * rows)
        ba = ba_ref[sl, :]
        beta = jax.nn.sigmoid(ba)
        t = ba + dtb
        g = neg_a * (jnp.maximum(t, 0.0) + jnp.log1p(jnp.exp(-jnp.abs(t))))
        hi = g.astype(BF16)
        r1 = g - hi.astype(F32)
        mid = r1.astype(BF16)
        lo = (r1 - mid.astype(F32)).astype(BF16)
        prefix = sum(jnp.dot(tri_prefix, p, preferred_element_type=F32) for p in (hi, mid, lo))
        suffix = sum(jnp.dot(tri_suffix, p, preferred_element_type=F32) for p in (hi, mid, lo))
        gc = jnp.where(lane < 2 * DN_V_HEADS, prefix, suffix)
        o_ref[sl, :] = jnp.where(is_beta, beta, gc)
        t_ref[sl, :] = prefix + suffix - g


def _dn_gates(ba, a_log, dt_bias, batch):
    zeros = jnp.zeros((DN_V_HEADS,), F32)
    alog = jnp.concatenate([zeros, a_log[0], zeros, a_log[1]]).reshape(1, LANES)
    dtb = jnp.concatenate([zeros, dt_bias[0], zeros, dt_bias[1]]).reshape(1, LANES)
    shape = jax.ShapeDtypeStruct((batch * SEG, LANES), F32)
    return pl.pallas_call(
        _gates_kernel,
        grid=(batch,),
        in_specs=[
            pl.BlockSpec((SEG, LANES), lambda b: (b, 0)),
            pl.BlockSpec((1, LANES), lambda b: (0, 0)),
            pl.BlockSpec((1, LANES), lambda b: (0, 0)),
        ],
        out_specs=[pl.BlockSpec((SEG, LANES), lambda b: (b, 0)),
                   pl.BlockSpec((SEG, LANES), lambda b: (b, 0))],
        out_shape=[shape, shape],
        compiler_params=_params(1),
        name="dn_gates",
    )(ba, alog, dtb)


def _gate_layouts(gates, totals, batch):
    c = DN_CHUNK
    g = gates.reshape(batch, SEG, 2, 2, DN_K_HEADS, 2)
    gcol = g.transpose(0, 4, 1, 2, 3, 5).reshape(batch, DN_K_HEADS, SEG, 8)
    gr = g.reshape(batch, DN_NCHUNK, c, 2, 2, DN_K_HEADS, 2)
    pair = gr.transpose(0, 5, 1, 3, 4, 6, 2).reshape(batch, DN_K_HEADS, DN_NCHUNK, 2, 2, 2 * c)
    tt = totals.reshape(batch, DN_NCHUNK, c, 2, 2, DN_K_HEADS, 2)[:, :, :, :, 1]
    tot_pair = tt.transpose(0, 4, 1, 3, 5, 2).reshape(batch, DN_K_HEADS, DN_NCHUNK, 2, 1, 2 * c)
    tot_head = jnp.broadcast_to(
        tt[:, :, 0].transpose(0, 3, 1, 2, 4)[..., None], (batch, DN_K_HEADS, DN_NCHUNK, 2, 2, 2 * c))
    pad = jnp.zeros((batch, DN_K_HEADS, DN_NCHUNK, 2, 3, 2 * c), F32)
    grow = jnp.concatenate([pair, tot_pair, tot_head, pad], axis=4)
    return gcol, grow


DN_PAIRS = 2 * DN_NCHUNK
DN_LEVELS = DN_CHUNK.bit_length() - 1
DN_PREP_UNROLL = 6
DN_INV_UNROLL = DN_PAIRS
DN_SOLVE_UNROLL = 12


DN_PAD = 8
DN_CTX_ROW = SEQ + 2 * DN_PAD
DN_STAGE_ROWS = SEG + 3 * DN_PAD


def _short_conv_silu(stage_ref, x_ref, cols, w, store):
    width = cols.stop - cols.start
    zeros = jnp.zeros((DN_PAD, width), F32)
    stage_ref[0:DN_PAD, 0:width] = zeros
    stage_ref[DN_PAD + SEQ:DN_CTX_ROW, 0:width] = zeros
    stage_ref[DN_CTX_ROW + CTX_LEN:DN_STAGE_ROWS, 0:width] = zeros
    stage_ref[DN_PAD:DN_PAD + SEQ, 0:width] = x_ref[0:SEQ, cols]
    stage_ref[DN_CTX_ROW:DN_CTX_ROW + CTX_LEN, 0:width] = x_ref[SEQ:SEG, cols]
    for start, n, out_row in ((DN_PAD, SEQ, 0), (DN_CTX_ROW, CTX_LEN, SEQ)):
        acc = None
        for tap in range(DN_CONV):
            d = tap - DN_CONV // 2
            term = stage_ref[start + d:start + d + n, 0:width] * w[tap:tap + 1, :]
            acc = term if acc is None else acc + term
        store(slice(out_row, out_row + n), _silu(acc))


def _l2norm(t):
    return t * lax.rsqrt(jnp.sum(t * t, axis=-1, keepdims=True) + NORM_EPS)


def _block_diag(a, left, right):
    return jnp.concatenate([a * left, a * right], axis=0)


def _deltanet_kernel(q_ref, k_ref, v_ref, z_ref, cwq_ref, cwk_ref, cwv_ref, gcol_ref, grow_ref,
                     nw_ref, y_ref,
                     qn_ref, kn_ref, vs_ref, o_ref, s_ref, l_ref, x_ref, lo_ref, kt_ref, u_ref, w_ref, stage_ref):
    c = DN_CHUNK
    nc = DN_NCHUNK
    dv = DN_DV

    def store_q(rows, t):
        qn_ref[rows, :] = (_l2norm(t) * DN_DK ** -0.5).astype(BF16)

    def store_k(rows, t):
        kn_ref[rows, :] = _l2norm(t).astype(BF16)

    _short_conv_silu(stage_ref, q_ref, slice(0, DN_DK), cwq_ref[...], store_q)
    _short_conv_silu(stage_ref, k_ref, slice(0, DN_DK), cwk_ref[...], store_k)
    for j in range(2):
        cols = slice(j * dv, (j + 1) * dv)

        def store_v(rows, t, cols=cols):
            vs_ref[rows, cols] = t.astype(BF16)

        _short_conv_silu(stage_ref, v_ref, cols, cwv_ref[:, cols], store_v)
    o_ref[...] = jnp.zeros_like(o_ref)
    s_ref[...] = jnp.zeros_like(s_ref)

    ri = lax.broadcasted_iota(jnp.int32, (c, 2 * c), 0)
    li = lax.broadcasted_iota(jnp.int32, (c, 2 * c), 1)
    cj = li & (c - 1)
    is_left = li < c
    left = is_left.astype(BF16)
    right = 1 - left
    eye2 = (ri == cj).astype(F32)
    incl = (ri >= cj, ri <= cj)
    strict = (ri > cj, ri < cj)

    def level_mask(lv):
        return ((ri >> (lv + 1)) == (cj >> (lv + 1))) & ((ri >> lv) != (cj >> lv))

    eye_k = (lax.broadcasted_iota(jnp.int32, (2 * c, 2 * c), 0)
             == lax.broadcasted_iota(jnp.int32, (2 * c, 2 * c), 1)).astype(BF16)
    zeros_half = jnp.zeros((c, dv), BF16)

    def rows_of(chunk):
        return pl.ds(pl.multiple_of(chunk * c, c), c)

    def diag2(a_l, a_r):
        return jnp.concatenate([jnp.concatenate([a_l, zeros_half], axis=1),
                                jnp.concatenate([zeros_half, a_r], axis=1)], axis=0)

    def prep(chunk):
        rows = rows_of(chunk)
        kc = kn_ref[rows, :]
        qc = qn_ref[rows, :]
        kk2 = jnp.concatenate([kc, kc], axis=0)
        sq = _dot_nt(jnp.concatenate([kc, qc], axis=0), kk2)
        kt2 = _dot_nt(eye_k, kk2)
        gcol = gcol_ref[rows, :]
        for d in range(2):
            p = d * nc + chunk
            gr = grow_ref[chunk, d]
            beta_col = jnp.where(is_left, gcol[:, 4 * d:4 * d + 1], gcol[:, 4 * d + 1:4 * d + 2])
            gc_col = jnp.where(is_left, gcol[:, 4 * d + 2:4 * d + 3], gcol[:, 4 * d + 3:4 * d + 4])
            gc_row = gr[1:2, :]
            tot_row = gr[2:3, :]
            decay = jnp.where(incl[d], jnp.exp(jnp.where(incl[d], gc_col - gc_row, 0.0)), 0.0)
            lmat = jnp.where(strict[d], sq[0:c, :] * beta_col * decay, 0.0)
            l_ref[p] = lmat.astype(BF16)
            x_ref[p] = eye2 - jnp.where(level_mask(0), lmat, 0.0)
            lo_ref[p] = jnp.concatenate([eye2 * jnp.exp(gc_row), sq[c:2 * c, :] * decay], axis=1).astype(BF16)
            kt_ref[p] = (kt2 * jnp.exp(tot_row - gc_row)).astype(BF16)

    def prep_body(i, carry):
        for u in range(DN_PREP_UNROLL):
            prep(i * DN_PREP_UNROLL + u)
        return carry

    lax.fori_loop(0, nc // DN_PREP_UNROLL, prep_body, 0)

    for lv in range(1, DN_LEVELS):
        lmask = level_mask(lv).astype(BF16)

        def inv_body(i, carry, lmask=lmask):
            stage = []
            for u in range(DN_INV_UNROLL):
                p = i * DN_INV_UNROLL + u
                x = x_ref[p]
                xb = x.astype(BF16)
                stage.append((p, x, xb, _dot(l_ref[p] * lmask, _block_diag(xb, left, right))))
            for p, x, xb, y in stage:
                x_ref[p] = x - _dot(xb, _block_diag(y.astype(BF16), left, right))
            return carry

        lax.fori_loop(0, DN_PAIRS // DN_INV_UNROLL, inv_body, 0)

    def solve_body(i, carry):
        for u in range(DN_SOLVE_UNROLL):
            p = i * DN_SOLVE_UNROLL + u
            d = p // nc
            chunk = p - d * nc
            rows = rows_of(chunk)
            gr = grow_ref[chunk, d]
            beta_row = gr[0:1, :]
            x = x_ref[p]
            tu = x * beta_row
            tw = x * (beta_row * jnp.exp(gr[1:2, :]))
            kc = kn_ref[rows, :]
            u_ref[p] = _dot(tu, diag2(vs_ref[rows, 0:dv], vs_ref[rows, dv:2 * dv]))
            w_ref[p] = _dot(tw, diag2(kc, kc)).astype(BF16)
        return carry

    lax.fori_loop(0, DN_PAIRS // DN_SOLVE_UNROLL, solve_body, 0)

    def scan_body(n, carry):
        n_ctx = nc - DN_LAT_CHUNKS
        chunk_f = jnp.where(n < n_ctx, DN_LAT_CHUNKS + n, n - n_ctx)
        chunk_b = nc - 1 - n
        stage = []
        for d, chunk in ((0, chunk_f), (1, chunk_b)):
            p = d * nc + chunk
            qc = qn_ref[rows_of(chunk), :]
            w2 = w_ref[p]
            s_old = [s_ref[2 * d + j] for j in range(2)]
            wq = [_dot(jnp.concatenate([w2[:, j * dv:(j + 1) * dv], qc], axis=0), s_old[j])
                  for j in range(2)]
            stage.append((d, chunk, p, s_old, wq))
        for d, chunk, p, s_old, wq in stage:
            rows = rows_of(chunk)
            u2 = u_ref[p]
            gr = grow_ref[chunk, d]
            vn = [(u2[:, j * dv:(j + 1) * dv] - wq[j][0:c, :]).astype(BF16) for j in range(2)]
            qs = [wq[j][c:2 * c, :].astype(BF16) for j in range(2)]
            rhs = jnp.concatenate([diag2(qs[0], qs[1]), diag2(vn[0], vn[1])], axis=0)
            o_ref[d, rows, :] += _dot(lo_ref[p], rhs)
            ds = _dot(kt_ref[p], diag2(vn[0], vn[1]))
            for j in range(2):
                s_ref[2 * d + j] = s_old[j] * jnp.exp(gr[3 + j:4 + j, :]) + ds[:, j * dv:(j + 1) * dv]
        return carry

    lax.fori_loop(0, nc, scan_body, 0)

    nw = nw_ref[...]
    for j in range(2):
        cols = slice(j * dv, (j + 1) * dv)
        o = o_ref[0, :, cols] + o_ref[1, :, cols]
        rms = o * lax.rsqrt(jnp.mean(o * o, axis=-1, keepdims=True) + NORM_EPS)
        y_ref[:, cols] = (rms * nw * _silu(z_ref[:, cols])).astype(BF16)


def _deltanet(p, conv_w, gcol, grow, norm_w, batch):
    kb = DN_QK // DN_DK
    vb = 2 * DN_QK // (2 * DN_DV)
    zb = DN_CONV_CH // (2 * DN_DV)
    pair = 2 * DN_DV
    c = DN_CHUNK
    return pl.pallas_call(
        _deltanet_kernel,
        grid=(batch, DN_K_HEADS),
        in_specs=[
            pl.BlockSpec((SEG, DN_DK), lambda b, h: (b, h)),
            pl.BlockSpec((SEG, DN_DK), lambda b, h: (b, kb + h)),
            pl.BlockSpec((SEG, pair), lambda b, h: (b, vb + h)),
            pl.BlockSpec((SEG, pair), lambda b, h: (b, zb + h)),
            pl.BlockSpec((DN_CONV, DN_DK), lambda b, h: (0, h)),
            pl.BlockSpec((DN_CONV, DN_DK), lambda b, h: (0, kb + h)),
            pl.BlockSpec((DN_CONV, pair), lambda b, h: (0, vb + h)),
            pl.BlockSpec((None, None, SEG, 8), lambda b, h: (b, h, 0, 0)),
            pl.BlockSpec((None, None, DN_NCHUNK, 2, 8, 2 * c), lambda b, h: (b, h, 0, 0, 0, 0)),
            pl.BlockSpec((1, DN_DV), lambda b, h: (0, 0)),
        ],
        out_specs=pl.BlockSpec((SEG, pair), lambda b, h: (b, h)),
        out_shape=jax.ShapeDtypeStruct((batch * SEG, DN_VW), BF16),
        scratch_shapes=[pltpu.VMEM((SEG, DN_DK), BF16),
                        pltpu.VMEM((SEG, DN_DK), BF16),
                        pltpu.VMEM((SEG, pair), BF16),
                        pltpu.VMEM((2, SEG, pair), F32),
                        pltpu.VMEM((4, DN_DK, DN_DV), F32),
                        pltpu.VMEM((DN_PAIRS, c, 2 * c), BF16),
                        pltpu.VMEM((DN_PAIRS, c, 2 * c), F32),
                        pltpu.VMEM((DN_PAIRS, c, 4 * c), BF16),
                        pltpu.VMEM((DN_PAIRS, 2 * c, 2 * c), BF16),
                        pltpu.VMEM((DN_PAIRS, c, pair), F32),
                        pltpu.VMEM((DN_PAIRS, c, pair), BF16),
                        pltpu.VMEM((DN_STAGE_ROWS, DN_DK), F32)],
        compiler_params=_params(2),
        name="deltanet",
    )(p, p, p, p, conv_w, conv_w, conv_w, gcol, grow, norm_w.reshape(1, DN_DV))


def kernel(x, c, ctx, c_ctx, mod_w, mod_b, ln_g, ln_b, ffn_w_in, ffn_w_out, ret_w_in, ret_log_decay,
           ret_w_out, dn_w_in, dn_conv_w, dn_a_log, dn_dt_bias, dn_norm_w, dn_w_out):
    batch = x.shape[0]
    xs = jnp.concatenate([x, ctx], axis=1).reshape(batch * SEG, D_MODEL)
    cond = jnp.concatenate([c, c_ctx[None, :]], axis=0)
    mod = _modulation(cond, mod_w, mod_b).reshape(DEPTH, batch + 1, N_MOD, D_MODEL)
    cos, sin = _axial_rotary()
    ffn_in = ffn_w_in.astype(BF16)
    ffn_out = ffn_w_out.astype(BF16)
    ret_in = _prep_ret_w_in(ret_w_in)
    ret_out = ret_w_out.astype(BF16)
    dn_in = dn_w_in.astype(BF16)
    dn_out = dn_w_out.astype(BF16)
    dn_main_blocks = (DN_CONV_CH + DN_VW) // PROJ_TN

    for i in range(DEPTH):
        j = i // N_MIXERS
        m = mod[i]
        xs = _ffn_step(xs, m, i, 0, ffn_in, ffn_out, ln_g[i, 0], ln_b[i, 0])
        if i % N_MIXERS == 0:
            p = _mixer_proj(xs, m, ret_in, j, PROJ_TN)
            y = _retention(p, ret_log_decay[j], cos, sin, batch)
            xs = _mixer_out(y, ret_out, j, xs, m, ln_g[i, 1], ln_b[i, 1])
        else:
            p = _mixer_proj(xs, m, dn_in, j, PROJ_TN, 0, dn_main_blocks)
            ba = _mixer_proj(xs, m, dn_in, j, LANES, (DN_CONV_CH + DN_VW) // LANES, 1)
            gates, totals = _dn_gates(ba, dn_a_log[j], dn_dt_bias[j], batch)
            gcol, grow = _gate_layouts(gates, totals, batch)
            y = _deltanet(p, dn_conv_w[j], gcol, grow, dn_norm_w[j], batch)
            xs = _mixer_out(y, dn_out, j, xs, m, ln_g[i, 1], ln_b[i, 1])
        xs = _ffn_step(xs, m, i, 1, ffn_in, ffn_out, ln_g[i, 2], ln_b[i, 2])

    return xs.reshape(batch, SEG, D_MODEL)[:, :SEQ]
```

```python
import functools

import jax
import jax.numpy as jnp
from jax import lax
from jax.experimental import pallas as pl
from jax.experimental.pallas import tpu as pltpu

D_MODEL = 2048
SEQ = 2048
DEPTH = 4
GRID_W = 64
CTX_LEN = 256
SEG = SEQ + CTX_LEN
N_MIXERS = 2
RET_HEADS = D_MODEL // 256
RET_DK = 256
RET_DV = 2 * RET_DK
RET_CHUNK = 128
RET_QK = RET_HEADS * RET_DK
RET_VW = RET_HEADS * RET_DV
RET_IN = 2 * RET_QK + 2 * RET_VW
ROPE_BASE = 10000.0
DN_K_HEADS = D_MODEL // 128
DN_V_HEADS = 2 * DN_K_HEADS
DN_DK = 128
DN_DV = 128
DN_CHUNK = 64
DN_CONV = 5
DN_QK = DN_K_HEADS * DN_DK
DN_VW = DN_V_HEADS * DN_DV
DN_CONV_CH = 2 * DN_QK + DN_VW
DN_IN = DN_CONV_CH + DN_VW + 4 * DN_V_HEADS
FFN_HIDDEN = 5504
N_MOD = 9
ALPHA = (2 * DEPTH) ** 0.25
LN_EPS = 1e-5
NORM_EPS = 1e-6

F32 = jnp.float32
BF16 = jnp.bfloat16

VMEM_LIMIT_BYTES = 56 * 1024 * 1024
LANES = 128

FFN_TH = 512
FFN_TM = 768
PROJ_TM = 1152
PROJ_TN = 1536
OUT_TM = 576
OUT_TK = 4096
MOD_TN = 1024

RET_NCHUNK = SEG // RET_CHUNK
RET_LAT_CHUNKS = SEQ // RET_CHUNK
DN_NCHUNK = SEG // DN_CHUNK
DN_LAT_CHUNKS = SEQ // DN_CHUNK
GATE_GROUP = 256


def _params(n_axes):
    return pltpu.CompilerParams(
        dimension_semantics=("arbitrary",) * n_axes,
        vmem_limit_bytes=VMEM_LIMIT_BYTES)


def _dot(a, b):
    return jnp.dot(a.astype(BF16), b.astype(BF16), preferred_element_type=F32)


def _dot_nt(a, b):
    return lax.dot_general(a.astype(BF16), b.astype(BF16), (((1,), (1,)), ((), ())),
                           preferred_element_type=F32)


def _dot_tn(a, b):
    return lax.dot_general(a.astype(BF16), b.astype(BF16), (((0,), (0,)), ((), ())),
                           preferred_element_type=F32)


def _silu(t):
    return t * jax.nn.sigmoid(t)


def _rows_are_ctx(block, tm):
    blocks_per_seg = SEG // tm
    r0 = (block % blocks_per_seg) * tm
    return r0 + lax.broadcasted_iota(jnp.int32, (tm, 1), 0) >= SEQ


def _pick_mod(is_ctx, mx_ref, mc_ref, r):
    return jnp.where(is_ctx, mc_ref[0, r:r + 1, :], mx_ref[0, r:r + 1, :])


def _layer_norm(z, g, b):
    mu = jnp.mean(z, axis=-1, keepdims=True)
    zc = z - mu
    var = jnp.mean(zc * zc, axis=-1, keepdims=True)
    return zc * lax.rsqrt(var + LN_EPS) * g + b


def _mod_kernel(c_ref, w_ref, b_ref, o_ref):
    a = _silu(c_ref[...])
    o_ref[0] = _dot(a, w_ref[0]) + b_ref[0]


def _modulation(cond, mod_w, mod_b):
    n = N_MOD * D_MODEL
    rows = cond.shape[0]
    return pl.pallas_call(
        _mod_kernel,
        grid=(DEPTH, n // MOD_TN),
        in_specs=[
            pl.BlockSpec((rows, D_MODEL), lambda l, j: (0, 0)),
            pl.BlockSpec((1, D_MODEL, MOD_TN), lambda l, j: (l, 0, j)),
            pl.BlockSpec((1, 1, MOD_TN), lambda l, j: (l, 0, j)),
        ],
        out_specs=pl.BlockSpec((1, rows, MOD_TN), lambda l, j: (l, 0, j)),
        out_shape=jax.ShapeDtypeStruct((DEPTH, rows, n), F32),
        compiler_params=_params(2),
        name="ada_modulation",
    )(cond, mod_w, mod_b.reshape(DEPTH, 1, n))


FFN_STEPS = -(-FFN_HIDDEN // FFN_TH)
FFN_TAIL = FFN_HIDDEN - (FFN_STEPS - 1) * FFN_TH


def _ffn_kernel(x_ref, mx_ref, mc_ref, wg_ref, wu_ref, wo_ref, g_ref, b_ref, o_ref, h_ref, acc_ref, *, sub):
    i = pl.program_id(0)
    k = pl.program_id(1)
    last = pl.num_programs(1) - 1
    is_ctx = _rows_are_ctx(i, FFN_TM)

    @pl.when(k == 0)
    def _():
        shift = _pick_mod(is_ctx, mx_ref, mc_ref, 3 * sub)
        scale = _pick_mod(is_ctx, mx_ref, mc_ref, 3 * sub + 1)
        h_ref[...] = (x_ref[...] * (1 + scale) + shift).astype(BF16)
        acc_ref[...] = jnp.zeros_like(acc_ref)

    def accumulate(wg, wu, wo):
        h = h_ref[...]
        gate = _dot(h, wg)
        up = _dot(h, wu)
        acc_ref[...] += _dot(_silu(gate) * up, wo)

    @pl.when(k < last)
    def _():
        accumulate(wg_ref[...], wu_ref[0, 0], wo_ref[0, 0])

    @pl.when(k == last)
    def _():
        accumulate(wg_ref[:, 0:FFN_TAIL], wu_ref[0, 0, :, FFN_TH - FFN_TAIL:FFN_TH],
                   wo_ref[0, 0, FFN_TH - FFN_TAIL:FFN_TH, :])
        res_gate = _pick_mod(is_ctx, mx_ref, mc_ref, 3 * sub + 2)
        z = ALPHA * x_ref[...] + 0.5 * res_gate * acc_ref[...]
        o_ref[...] = _layer_norm(z, g_ref[...], b_ref[...])


def _ffn_step(xs, mod, layer, half, w_in, w_out, g, b):
    t = xs.shape[0]
    one = pl.Element(1)
    blocks_per_seg = SEG // FFN_TM
    n_ctx_row = mod.shape[0] - 1
    return pl.pallas_call(
        functools.partial(_ffn_kernel, sub=2 * half),
        grid=(t // FFN_TM, FFN_STEPS),
        in_specs=[
            pl.BlockSpec((FFN_TM, D_MODEL), lambda i, k: (i, 0)),
            pl.BlockSpec((1, N_MOD, D_MODEL), lambda i, k: (i // blocks_per_seg, 0, 0)),
            pl.BlockSpec((1, N_MOD, D_MODEL), lambda i, k: (n_ctx_row, 0, 0)),
            pl.BlockSpec((None, None, D_MODEL, FFN_TH), lambda i, k: (layer, half, 0, k)),
            pl.BlockSpec((one, one, pl.Element(D_MODEL), pl.Element(FFN_TH)),
                         lambda i, k: (layer, half, 0, jnp.minimum((FFN_HIDDEN + k * FFN_TH) // LANES,
                                                      (2 * FFN_HIDDEN - FFN_TH) // LANES) * LANES)),
            pl.BlockSpec((one, one, pl.Element(FFN_TH), pl.Element(D_MODEL)),
                         lambda i, k: (layer, half, jnp.minimum(k * FFN_TH // LANES,
                                                                (FFN_HIDDEN - FFN_TH) // LANES) * LANES, 0)),
            pl.BlockSpec((1, D_MODEL), lambda i, k: (0, 0)),
            pl.BlockSpec((1, D_MODEL), lambda i, k: (0, 0)),
        ],
        out_specs=pl.BlockSpec((FFN_TM, D_MODEL), lambda i, k: (i, 0)),
        out_shape=jax.ShapeDtypeStruct((t, D_MODEL), F32),
        scratch_shapes=[pltpu.VMEM((FFN_TM, D_MODEL), BF16),
                        pltpu.VMEM((FFN_TM, D_MODEL), F32)],
        compiler_params=_params(2),
        name="ffn_step",
    )(xs, mod, mod, w_in, w_in, w_out, g.reshape(1, D_MODEL), b.reshape(1, D_MODEL))


def _proj_kernel(x_ref, mx_ref, mc_ref, w_ref, o_ref, h_ref):
    i = pl.program_id(0)

    @pl.when(pl.program_id(1) == 0)
    def _():
        is_ctx = _rows_are_ctx(i, PROJ_TM)
        shift = _pick_mod(is_ctx, mx_ref, mc_ref, 3)
        scale = _pick_mod(is_ctx, mx_ref, mc_ref, 4)
        h_ref[...] = (x_ref[...] * (1 + scale) + shift).astype(BF16)

    o_ref[...] = jnp.dot(h_ref[...], w_ref[...], preferred_element_type=F32)


def _mixer_proj(xs, mod, w, layer, tn, first_block=0, n_blocks=None):
    t = xs.shape[0]
    if n_blocks is None:
        n_blocks = w.shape[2] // tn
    blocks_per_seg = SEG // PROJ_TM
    n_ctx_row = mod.shape[0] - 1
    return pl.pallas_call(
        _proj_kernel,
        grid=(t // PROJ_TM, n_blocks),
        in_specs=[
            pl.BlockSpec((PROJ_TM, D_MODEL), lambda i, j: (i, 0)),
            pl.BlockSpec((1, N_MOD, D_MODEL), lambda i, j: (i // blocks_per_seg, 0, 0)),
            pl.BlockSpec((1, N_MOD, D_MODEL), lambda i, j: (n_ctx_row, 0, 0)),
            pl.BlockSpec((None, D_MODEL, tn), lambda i, j: (layer, 0, first_block + j)),
        ],
        out_specs=pl.BlockSpec((PROJ_TM, tn), lambda i, j: (i, j)),
        out_shape=jax.ShapeDtypeStruct((t, n_blocks * tn), F32),
        scratch_shapes=[pltpu.VMEM((PROJ_TM, D_MODEL), BF16)],
        compiler_params=_params(2),
        name="mixer_proj",
    )(xs, mod, mod, w)


def _out_kernel(y_ref, w_ref, x_ref, mx_ref, mc_ref, g_ref, b_ref, o_ref, acc_ref):
    i = pl.program_id(0)
    k = pl.program_id(1)

    @pl.when(k == 0)
    def _():
        acc_ref[...] = jnp.zeros_like(acc_ref)

    acc_ref[...] += jnp.dot(y_ref[...], w_ref[...], preferred_element_type=F32)

    @pl.when(k == pl.num_programs(1) - 1)
    def _():
        is_ctx = _rows_are_ctx(i, OUT_TM)
        gate = _pick_mod(is_ctx, mx_ref, mc_ref, 5)
        z = ALPHA * x_ref[...] + gate * acc_ref[...]
        o_ref[...] = _layer_norm(z, g_ref[...], b_ref[...])


def _mixer_out(y, w, layer, xs, mod, g, b):
    t = xs.shape[0]
    kdim = w.shape[1]
    blocks_per_seg = SEG // OUT_TM
    n_ctx_row = mod.shape[0] - 1
    return pl.pallas_call(
        _out_kernel,
        grid=(t // OUT_TM, kdim // OUT_TK),
        in_specs=[
            pl.BlockSpec((OUT_TM, OUT_TK), lambda i, k: (i, k)),
            pl.BlockSpec((None, OUT_TK, D_MODEL), lambda i, k: (layer, k, 0),
                         pipeline_mode=pl.Buffered(1)),
            pl.BlockSpec((OUT_TM, D_MODEL), lambda i, k: (i, 0)),
            pl.BlockSpec((1, N_MOD, D_MODEL), lambda i, k: (i // blocks_per_seg, 0, 0)),
            pl.BlockSpec((1, N_MOD, D_MODEL), lambda i, k: (n_ctx_row, 0, 0)),
            pl.BlockSpec((1, D_MODEL), lambda i, k: (0, 0)),
            pl.BlockSpec((1, D_MODEL), lambda i, k: (0, 0)),
        ],
        out_specs=pl.BlockSpec((OUT_TM, D_MODEL), lambda i, k: (i, 0)),
        out_shape=jax.ShapeDtypeStruct((t, D_MODEL), F32),
        scratch_shapes=[pltpu.VMEM((OUT_TM, D_MODEL), F32)],
        compiler_params=_params(2),
        name="mixer_out",
    )(y, w, xs, mod, mod, g.reshape(1, D_MODEL), b.reshape(1, D_MODEL))


def _retention_kernel(ld_ref, q_ref, k_ref, v_ref, g_ref, cos_ref, sin_ref, y_ref,
                      qs_ref, ks_ref, o_ref, sf_ref, sb_ref):
    h = pl.program_id(1)
    lg_f = ld_ref[0, h]
    lg_b = ld_ref[1, h]
    half = RET_DK // 2
    c = RET_CHUNK

    def rows_of(chunk):
        return pl.ds(pl.multiple_of(chunk * c, c), c)

    def rotate(n, carry):
        r = rows_of(n)
        cos = cos_ref[r, :]
        sin = sin_ref[r, :]
        for src, dst, scale in ((q_ref, qs_ref, 1.0), (k_ref, ks_ref, RET_DK ** -0.5)):
            t1 = src[r, 0:half] * scale
            t2 = src[r, half:RET_DK] * scale
            dst[r, 0:half] = t1 * cos - t2 * sin
            dst[r, half:RET_DK] = t1 * sin + t2 * cos
        return carry

    lax.fori_loop(0, RET_LAT_CHUNKS, rotate, 0)
    qs_ref[SEQ:SEG, :] = q_ref[SEQ:SEG, :]
    ks_ref[SEQ:SEG, :] = k_ref[SEQ:SEG, :] * RET_DK ** -0.5

    ri = lax.broadcasted_iota(jnp.int32, (c, c), 0)
    ci = lax.broadcasted_iota(jnp.int32, (c, c), 1)
    rel = (ri - ci).astype(F32)
    intra = jnp.where(ri >= ci, jnp.exp(jnp.maximum(rel, 0.0) * lg_f), 0.0) \
        + jnp.where(ri <= ci, jnp.exp(jnp.maximum(-rel, 0.0) * lg_b), 0.0)
    idx = lax.broadcasted_iota(jnp.int32, (c, 1), 0).astype(F32)
    qdec_f = jnp.exp((idx + 1.0) * lg_f)
    kdec_f = jnp.exp((c - 1.0 - idx) * lg_f)
    qdec_b = jnp.exp((c - idx) * lg_b)
    kdec_b = jnp.exp(idx * lg_b)
    cdec_f = jnp.exp(c * lg_f)
    cdec_b = jnp.exp(c * lg_b)

    sf_ref[...] = jnp.zeros_like(sf_ref)
    sb_ref[...] = jnp.zeros_like(sb_ref)

    o_ref[...] = jnp.zeros_like(o_ref)

    def visit(n):
        n_ctx = RET_NCHUNK - RET_LAT_CHUNKS
        cf = jnp.where(n < n_ctx, RET_LAT_CHUNKS + n, n - n_ctx)
        cb = RET_NCHUNK - 1 - n
        rf = rows_of(cf)
        rb = rows_of(cb)
        qf = qs_ref[rf, :]
        kf = ks_ref[rf, :]
        vf = v_ref[rf, :]
        qb = qs_ref[rb, :]
        kb = ks_ref[rb, :]
        vb = v_ref[rb, :]
        scores = _dot_nt(qf, kf) * intra
        kv_f = _dot_tn(kf * kdec_f, vf)
        kv_b = _dot_tn(kb * kdec_b, vb)
        return rf, rb, qf, qb, _dot(scores, vf), kv_f, kv_b

    def body(i, carry):
        pre = [visit(2 * i), visit(2 * i + 1)]
        for rf, rb, qf, qb, intra_o, kv_f, kv_b in pre:
            sf = sf_ref[...]
            sb = sb_ref[...]
            o_ref[rf, :] += intra_o + _dot(qf * qdec_f, sf)
            o_ref[rb, :] += _dot(qb * qdec_b, sb)
            sf_ref[...] = sf * cdec_f + kv_f
            sb_ref[...] = sb * cdec_b + kv_b
        return carry

    lax.fori_loop(0, RET_NCHUNK // 2, body, 0)

    def finish(n, carry):
        r = rows_of(n)
        o = o_ref[r, :]
        mu = jnp.mean(o, axis=-1, keepdims=True)
        oc = o - mu
        var = jnp.mean(oc * oc, axis=-1, keepdims=True)
        y_ref[r, :] = (oc * lax.rsqrt(var + LN_EPS) * _silu(g_ref[r, :])).astype(BF16)
        return carry

    lax.fori_loop(0, RET_NCHUNK, finish, 0)


def _retention(p, log_decay, cos, sin, batch):
    kb = RET_QK // RET_DK
    vb = 2 * RET_QK // RET_DV
    gb = vb + RET_HEADS
    half = RET_DK // 2
    return pl.pallas_call(
        _retention_kernel,
        grid=(batch, RET_HEADS),
        in_specs=[
            pl.BlockSpec(memory_space=pltpu.SMEM),
            pl.BlockSpec((SEG, RET_DK), lambda b, h: (b, h)),
            pl.BlockSpec((SEG, RET_DK), lambda b, h: (b, kb + h)),
            pl.BlockSpec((SEG, RET_DV), lambda b, h: (b, vb + h)),
            pl.BlockSpec((SEG, RET_DV), lambda b, h: (b, gb + h)),
            pl.BlockSpec((SEQ, half), lambda b, h: (0, 0)),
            pl.BlockSpec((SEQ, half), lambda b, h: (0, 0)),
        ],
        out_specs=pl.BlockSpec((SEG, RET_DV), lambda b, h: (b, h)),
        out_shape=jax.ShapeDtypeStruct((batch * SEG, RET_VW), BF16),
        scratch_shapes=[pltpu.VMEM((SEG, RET_DK), F32),
                        pltpu.VMEM((SEG, RET_DK), F32),
                        pltpu.VMEM((SEG, RET_DV), F32),
                        pltpu.VMEM((RET_DK, RET_DV), F32),
                        pltpu.VMEM((RET_DK, RET_DV), F32)],
        compiler_params=_params(2),
        name="retention",
    )(log_decay, p, p, p, p, cos, sin)


def _axial_rotary():
    rows = SEQ // GRID_W
    pos_r = jnp.repeat(jnp.arange(rows), GRID_W).astype(F32)
    pos_c = jnp.tile(jnp.arange(GRID_W), rows).astype(F32)
    half = RET_DK // 2
    inv = ROPE_BASE ** (-jnp.arange(0, half, 2, dtype=F32) / half)
    ang = jnp.concatenate([pos_r[:, None] * inv, pos_c[:, None] * inv], -1)
    return jnp.cos(ang), jnp.sin(ang)


def _ret_w_kernel(w_ref, o_ref):
    n = RET_DK
    r = lax.broadcasted_iota(jnp.int32, (n, n), 0)
    col = lax.broadcasted_iota(jnp.int32, (n, n), 1)
    src = jnp.where(col < n // 2, 2 * col, 2 * (col - n // 2) + 1)
    is_qk = pl.program_id(1) < 2 * RET_HEADS
    pmat = (r == jnp.where(is_qk, src, col)).astype(BF16)
    o_ref[0] = _dot(w_ref[0], pmat).astype(BF16)


def _prep_ret_w_in(w):
    layers = w.shape[0]
    return pl.pallas_call(
        _ret_w_kernel,
        grid=(layers, RET_IN // RET_DK),
        in_specs=[pl.BlockSpec((1, D_MODEL, RET_DK), lambda l, j: (l, 0, j))],
        out_specs=pl.BlockSpec((1, D_MODEL, RET_DK), lambda l, j: (l, 0, j)),
        out_shape=jax.ShapeDtypeStruct((layers, D_MODEL, RET_IN), BF16),
        compiler_params=_params(2),
        name="ret_w_in_prep",
    )(w)


def _gates_kernel(ba_ref, alog_ref, dtb_ref, o_ref, t_ref):
    rows = GATE_GROUP
    lane = lax.broadcasted_iota(jnp.int32, (1, LANES), 1)
    is_beta = (lane // DN_V_HEADS) % 2 == 0
    ri = lax.broadcasted_iota(jnp.int32, (rows, rows), 0)
    ci = lax.broadcasted_iota(jnp.int32, (rows, rows), 1)
    same = (ri // DN_CHUNK) == (ci // DN_CHUNK)
    tri_prefix = jnp.where(same & (ri >= ci), 1.0, 0.0).astype(BF16)
    tri_suffix = jnp.where(same & (ri <= ci), 1.0, 0.0).astype(BF16)
    neg_a = -jnp.exp(alog_ref[...])
    dtb = dtb_ref[...]
    for grp in range(SEG // rows):
        sl = slice(grp * rows, (grp + 1) * rows)
        ba = ba_ref[sl, :]
        beta = jax.nn.sigmoid(ba)
        t = ba + dtb
        g = neg_a * (jnp.maximum(t, 0.0) + jnp.log1p(jnp.exp(-jnp.abs(t))))
        hi = g.astype(BF16)
        r1 = g - hi.astype(F32)
        mid = r1.astype(BF16)
        lo = (r1 - mid.astype(F32)).astype(BF16)
        prefix = sum(jnp.dot(tri_prefix, p, preferred_element_type=F32) for p in (hi, mid, lo))
        suffix = sum(jnp.dot(tri_suffix, p, preferred_element_type=F32) for p in (hi, mid, lo))
        gc = jnp.where(lane < 2 * DN_V_HEADS, prefix, suffix)
        o_ref[sl, :] = jnp.where(is_beta, beta, gc)
        t_ref[sl, :] = prefix + suffix - g


def _dn_gates(ba, a_log, dt_bias, batch):
    zeros = jnp.zeros((DN_V_HEADS,), F32)
    alog = jnp.concatenate([zeros, a_log[0], zeros, a_log[1]]).reshape(1, LANES)
    dtb = jnp.concatenate([zeros, dt_bias[0], zeros, dt_bias[1]]).reshape(1, LANES)
    shape = jax.ShapeDtypeStruct((batch * SEG, LANES), F32)
    return pl.pallas_call(
        _gates_kernel,
        grid=(batch,),
        in_specs=[
            pl.BlockSpec((SEG, LANES), lambda b: (b, 0)),
            pl.BlockSpec((1, LANES), lambda b: (0, 0)),
            pl.BlockSpec((1, LANES), lambda b: (0, 0)),
        ],
        out_specs=[pl.BlockSpec((SEG, LANES), lambda b: (b, 0)),
                   pl.BlockSpec((SEG, LANES), lambda b: (b, 0))],
        out_shape=[shape, shape],
        compiler_params=_params(1),
        name="dn_gates",
    )(ba, alog, dtb)


def _gate_layouts(gates, totals, batch):
    c = DN_CHUNK
    g = gates.reshape(batch, SEG, 2, 2, DN_K_HEADS, 2)
    gcol = g.transpose(0, 4, 1, 2, 3, 5).reshape(batch, DN_K_HEADS, SEG, 8)
    gr = g.reshape(batch, DN_NCHUNK, c, 2, 2, DN_K_HEADS, 2)
    pair = gr.transpose(0, 5, 1, 3, 4, 6, 2).reshape(batch, DN_K_HEADS, DN_NCHUNK, 2, 2, 2 * c)
    tt = totals.reshape(batch, DN_NCHUNK, c, 2, 2, DN_K_HEADS, 2)[:, :, :, :, 1]
    tot_pair = tt.transpose(0, 4, 1, 3, 5, 2).reshape(batch, DN_K_HEADS, DN_NCHUNK, 2, 1, 2 * c)
    tot_head = jnp.broadcast_to(
        tt[:, :, 0].transpose(0, 3, 1, 2, 4)[..., None], (batch, DN_K_HEADS, DN_NCHUNK, 2, 2, 2 * c))
    pad = jnp.zeros((batch, DN_K_HEADS, DN_NCHUNK, 2, 3, 2 * c), F32)
    grow = jnp.concatenate([pair, tot_pair, tot_head, pad], axis=4)
    return gcol, grow


DN_PAIRS = 2 * DN_NCHUNK
DN_LEVELS = DN_CHUNK.bit_length() - 1
DN_PREP_UNROLL = 6
DN_INV_UNROLL = DN_PAIRS
DN_SOLVE_UNROLL = 18


DN_PAD = 8
DN_CTX_ROW = SEQ + 2 * DN_PAD
DN_STAGE_ROWS = SEG + 3 * DN_PAD


def _short_conv_silu(stage_ref, x_ref, cols, w, store):
    width = cols.stop - cols.start
    zeros = jnp.zeros((DN_PAD, width), F32)
    stage_ref[0:DN_PAD, 0:width] = zeros
    stage_ref[DN_PAD + SEQ:DN_CTX_ROW, 0:width] = zeros
    stage_ref[DN_CTX_ROW + CTX_LEN:DN_STAGE_ROWS, 0:width] = zeros
    stage_ref[DN_PAD:DN_PAD + SEQ, 0:width] = x_ref[0:SEQ, cols]
    stage_ref[DN_CTX_ROW:DN_CTX_ROW + CTX_LEN, 0:width] = x_ref[SEQ:SEG, cols]
    for start, n, out_row in ((DN_PAD, SEQ, 0), (DN_CTX_ROW, CTX_LEN, SEQ)):
        acc = None
        for tap in range(DN_CONV):
            d = tap - DN_CONV // 2
            term = stage_ref[start + d:start + d + n, 0:width] * w[tap:tap + 1, :]
            acc = term if acc is None else acc + term
        store(slice(out_row, out_row + n), _silu(acc))


def _l2norm(t):
    return t * lax.rsqrt(jnp.sum(t * t, axis=-1, keepdims=True) + NORM_EPS)


def _block_diag(a, left, right):
    return jnp.concatenate([a * left, a * right], axis=0)


def _deltanet_kernel(q_ref, k_ref, v_ref, z_ref, cwq_ref, cwk_ref, cwv_ref, gcol_ref, grow_ref,
                     nw_ref, y_ref,
                     qn_ref, kn_ref, vs_ref, o_ref, s_ref, l_ref, x_ref, lo_ref, kt_ref, u_ref, w_ref, stage_ref):
    c = DN_CHUNK
    nc = DN_NCHUNK
    dv = DN_DV

    def store_q(rows, t):
        qn_ref[rows, :] = (_l2norm(t) * DN_DK ** -0.5).astype(BF16)

    def store_k(rows, t):
        kn_ref[rows, :] = _l2norm(t).astype(BF16)

    _short_conv_silu(stage_ref, q_ref, slice(0, DN_DK), cwq_ref[...], store_q)
    _short_conv_silu(stage_ref, k_ref, slice(0, DN_DK), cwk_ref[...], store_k)
    for j in range(2):
        cols = slice(j * dv, (j + 1) * dv)

        def store_v(rows, t, cols=cols):
            vs_ref[rows, cols] = t.astype(BF16)

        _short_conv_silu(stage_ref, v_ref, cols, cwv_ref[:, cols], store_v)
    o_ref[...] = jnp.zeros_like(o_ref)
    s_ref[...] = jnp.zeros_like(s_ref)

    ri = lax.broadcasted_iota(jnp.int32, (c, 2 * c), 0)
    li = lax.broadcasted_iota(jnp.int32, (c, 2 * c), 1)
    cj = li & (c - 1)
    is_left = li < c
    left = is_left.astype(BF16)
    right = 1 - left
    eye2 = (ri == cj).astype(F32)
    incl = (ri >= cj, ri <= cj)
    strict = (ri > cj, ri < cj)

    def level_mask(lv):
        return ((ri >> (lv + 1)) == (cj >> (lv + 1))) & ((ri >> lv) != (cj >> lv))

    sel_r = lax.broadcasted_iota(jnp.int32, (24, 8 * c), 0) % 8
    sel_l = lax.broadcasted_iota(jnp.int32, (24, 8 * c), 1)
    gate_sel = (sel_r == 2 * (sel_l // (2 * c)) + (sel_l // c) % 2).astype(BF16)
    eye_k = (lax.broadcasted_iota(jnp.int32, (2 * c, 2 * c), 0)
             == lax.broadcasted_iota(jnp.int32, (2 * c, 2 * c), 1)).astype(BF16)
    zeros_half = jnp.zeros((c, dv), BF16)

    def rows_of(chunk):
        return pl.ds(pl.multiple_of(chunk * c, c), c)

    def diag2(a_l, a_r):
        return jnp.concatenate([jnp.concatenate([a_l, zeros_half], axis=1),
                                jnp.concatenate([zeros_half, a_r], axis=1)], axis=0)

    def prep(chunk):
        rows = rows_of(chunk)
        kc = kn_ref[rows, :]
        qc = qn_ref[rows, :]
        kk2 = jnp.concatenate([kc, kc], axis=0)
        sq = _dot_nt(jnp.concatenate([kc, qc], axis=0), kk2)
        kt2 = _dot_nt(eye_k, kk2)
        g8 = gcol_ref[rows, :]
        g_hi = g8.astype(BF16)
        g_r1 = g8 - g_hi.astype(F32)
        g_mid = g_r1.astype(BF16)
        g_lo = (g_r1 - g_mid.astype(F32)).astype(BF16)
        cols_bc = _dot(jnp.concatenate([g_hi, g_mid, g_lo], axis=1), gate_sel)
        for d in range(2):
            p = d * nc + chunk
            gr = grow_ref[chunk, d]
            beta_col = cols_bc[:, (4 * d) * c:(4 * d + 2) * c]
            gc_col = cols_bc[:, (4 * d + 2) * c:(4 * d + 4) * c]
            gc_row = gr[1:2, :]
            tot_row = gr[2:3, :]
            decay = jnp.where(incl[d], jnp.exp(jnp.where(incl[d], gc_col - gc_row, 0.0)), 0.0)
            lmat = jnp.where(strict[d], sq[0:c, :] * beta_col * decay, 0.0)
            l_ref[p] = lmat.astype(BF16)
            x_ref[p] = eye2 - jnp.where(level_mask(0), lmat, 0.0)
            lo_ref[p] = jnp.concatenate([eye2 * jnp.exp(gc_row), sq[c:2 * c, :] * decay], axis=1).astype(BF16)
            kt_ref[p] = (kt2 * jnp.exp(tot_row - gc_row)).astype(BF16)

    def prep_body(i, carry):
        for u in range(DN_PREP_UNROLL):
            prep(i * DN_PREP_UNROLL + u)
        return carry

    lax.fori_loop(0, nc // DN_PREP_UNROLL, prep_body, 0)

    for lv in range(1, DN_LEVELS):
        lmask = level_mask(lv).astype(BF16)

        def inv_body(i, carry, lmask=lmask):
            stage = []
            for u in range(DN_INV_UNROLL):
                p = i * DN_INV_UNROLL + u
                x = x_ref[p]
                xb = x.astype(BF16)
                stage.append((p, x, xb, _dot(l_ref[p] * lmask, _block_diag(xb, left, right))))
            for p, x, xb, y in stage:
                x_ref[p] = x - _dot(xb, _block_diag(y.astype(BF16), left, right))
            return carry

        lax.fori_loop(0, DN_PAIRS // DN_INV_UNROLL, inv_body, 0)

    def solve_body(i, carry):
        stage = []
        for u in range(DN_SOLVE_UNROLL):
            p = i * DN_SOLVE_UNROLL + u
            d = p // nc
            chunk = p - d * nc
            rows = rows_of(chunk)
            gr = grow_ref[chunk, d]
            beta_row = gr[0:1, :]
            x = x_ref[p]
            tu = x * beta_row
            tw = x * (beta_row * jnp.exp(gr[1:2, :]))
            kc = kn_ref[rows, :]
            stage.append((p, _dot(tu, diag2(vs_ref[rows, 0:dv], vs_ref[rows, dv:2 * dv])),
                          _dot(tw, diag2(kc, kc))))
        for p, u2, w2 in stage:
            u_ref[p] = u2
            w_ref[p] = w2.astype(BF16)
        return carry

    lax.fori_loop(0, DN_PAIRS // DN_SOLVE_UNROLL, solve_body, 0)

    def scan_body(n, carry):
        n_ctx = nc - DN_LAT_CHUNKS
        chunk_f = jnp.where(n < n_ctx, DN_LAT_CHUNKS + n, n - n_ctx)
        chunk_b = nc - 1 - n
        stage = []
        for d, chunk in ((0, chunk_f), (1, chunk_b)):
            p = d * nc + chunk
            qc = qn_ref[rows_of(chunk), :]
            w2 = w_ref[p]
            s_old = [s_ref[2 * d + j] for j in range(2)]
            wq = [_dot(jnp.concatenate([w2[:, j * dv:(j + 1) * dv], qc], axis=0), s_old[j])
                  for j in range(2)]
            stage.append((d, chunk, p, s_old, wq))
        for d, chunk, p, s_old, wq in stage:
            rows = rows_of(chunk)
            u2 = u_ref[p]
            gr = grow_ref[chunk, d]
            vn = [(u2[:, j * dv:(j + 1) * dv] - wq[j][0:c, :]).astype(BF16) for j in range(2)]
            qs = [wq[j][c:2 * c, :].astype(BF16) for j in range(2)]
            rhs = jnp.concatenate([diag2(qs[0], qs[1]), diag2(vn[0], vn[1])], axis=0)
            o_ref[d, rows, :] += _dot(lo_ref[p], rhs)
            ds = _dot(kt_ref[p], diag2(vn[0], vn[1]))
            for j in range(2):
                s_ref[2 * d + j] = s_old[j] * jnp.exp(gr[3 + j:4 + j, :]) + ds[:, j * dv:(j + 1) * dv]
        return carry

    lax.fori_loop(0, nc, scan_body, 0)

    nw = nw_ref[...]
    for j in range(2):
        cols = slice(j * dv, (j + 1) * dv)
        o = o_ref[0, :, cols] + o_ref[1, :, cols]
        rms = o * lax.rsqrt(jnp.mean(o * o, axis=-1, keepdims=True) + NORM_EPS)
        y_ref[:, cols] = (rms * nw * _silu(z_ref[:, cols])).astype(BF16)


def _deltanet(p, conv_w, gcol, grow, norm_w, batch):
    kb = DN_QK // DN_DK
    vb = 2 * DN_QK // (2 * DN_DV)
    zb = DN_CONV_CH // (2 * DN_DV)
    pair = 2 * DN_DV
    c = DN_CHUNK
    return pl.pallas_call(
        _deltanet_kernel,
        grid=(batch, DN_K_HEADS),
        in_specs=[
            pl.BlockSpec((SEG, DN_DK), lambda b, h: (b, h)),
            pl.BlockSpec((SEG, DN_DK), lambda b, h: (b, kb + h)),
            pl.BlockSpec((SEG, pair), lambda b, h: (b, vb + h)),
            pl.BlockSpec((SEG, pair), lambda b, h: (b, zb + h)),
            pl.BlockSpec((DN_CONV, DN_DK), lambda b, h: (0, h)),
            pl.BlockSpec((DN_CONV, DN_DK), lambda b, h: (0, kb + h)),
            pl.BlockSpec((DN_CONV, pair), lambda b, h: (0, vb + h)),
            pl.BlockSpec((None, None, SEG, 8), lambda b, h: (b, h, 0, 0)),
            pl.BlockSpec((None, None, DN_NCHUNK, 2, 8, 2 * c), lambda b, h: (b, h, 0, 0, 0, 0)),
            pl.BlockSpec((1, DN_DV), lambda b, h: (0, 0)),
        ],
        out_specs=pl.BlockSpec((SEG, pair), lambda b, h: (b, h)),
        out_shape=jax.ShapeDtypeStruct((batch * SEG, DN_VW), BF16),
        scratch_shapes=[pltpu.VMEM((SEG, DN_DK), BF16),
                        pltpu.VMEM((SEG, DN_DK), BF16),
                        pltpu.VMEM((SEG, pair), BF16),
                        pltpu.VMEM((2, SEG, pair), F32),
                        pltpu.VMEM((4, DN_DK, DN_DV), F32),
                        pltpu.VMEM((DN_PAIRS, c, 2 * c), BF16),
                        pltpu.VMEM((DN_PAIRS, c, 2 * c), F32),
                        pltpu.VMEM((DN_PAIRS, c, 4 * c), BF16),
                        pltpu.VMEM((DN_PAIRS, 2 * c, 2 * c), BF16),
                        pltpu.VMEM((DN_PAIRS, c, pair), F32),
                        pltpu.VMEM((DN_PAIRS, c, pair), BF16),
                        pltpu.VMEM((DN_STAGE_ROWS, DN_DK), F32)],
        compiler_params=_params(2),
        name="deltanet",
    )(p, p, p, p, conv_w, conv_w, conv_w, gcol, grow, norm_w.reshape(1, DN_DV))


def kernel(x, c, ctx, c_ctx, mod_w, mod_b, ln_g, ln_b, ffn_w_in, ffn_w_out, ret_w_in, ret_log_decay,
           ret_w_out, dn_w_in, dn_conv_w, dn_a_log, dn_dt_bias, dn_norm_w, dn_w_out):
    batch = x.shape[0]
    xs = jnp.concatenate([x, ctx], axis=1).reshape(batch * SEG, D_MODEL)
    cond = jnp.concatenate([c, c_ctx[None, :]], axis=0)
    mod = _modulation(cond, mod_w, mod_b).reshape(DEPTH, batch + 1, N_MOD, D_MODEL)
    cos, sin = _axial_rotary()
    ffn_in = ffn_w_in.astype(BF16)
    ffn_out = ffn_w_out.astype(BF16)
    ret_in = _prep_ret_w_in(ret_w_in)
    ret_out = ret_w_out.astype(BF16)
    dn_in = dn_w_in.astype(BF16)
    dn_out = dn_w_out.astype(BF16)
    dn_main_blocks = (DN_CONV_CH + DN_VW) // PROJ_TN

    for i in range(DEPTH):
        j = i // N_MIXERS
        m = mod[i]
        xs = _ffn_step(xs, m, i, 0, ffn_in, ffn_out, ln_g[i, 0], ln_b[i, 0])
        if i % N_MIXERS == 0:
            p = _mixer_proj(xs, m, ret_in, j, PROJ_TN)
            y = _retention(p, ret_log_decay[j], cos, sin, batch)
            xs = _mixer_out(y, ret_out, j, xs, m, ln_g[i, 1], ln_b[i, 1])
        else:
            p = _mixer_proj(xs, m, dn_in, j, PROJ_TN, 0, dn_main_blocks)
            ba = _mixer_proj(xs, m, dn_in, j, LANES, (DN_CONV_CH + DN_VW) // LANES, 1)
            gates, totals = _dn_gates(ba, dn_a_log[j], dn_dt_bias[j], batch)
            gcol, grow = _gate_layouts(gates, totals, batch)
            y = _deltanet(p, dn_conv_w[j], gcol, grow, dn_norm_w[j], batch)
            xs = _mixer_out(y, dn_out, j, xs, m, ln_g[i, 1], ln_b[i, 1])
        xs = _ffn_step(xs, m, i, 1, ffn_in, ffn_out, ln_g[i, 2], ln_b[i, 2])

    return xs.reshape(batch, SEG, D_MODEL)[:, :SEQ]
```
